```python
import jax, jax.numpy as jnp
from jax import lax
import numpy as np

D_MODEL = 1024
BATCH = 8
SEQ = 4096
DEPTH = 1
DEC_BATCH = 128
DEC_SEQ = 8
PAST_LEN = 8192
PAGE_SIZE = 128

SB_HEADS = 8
SB_DIM = 64
SB_WIDTH = SB_HEADS * SB_DIM
RET_HEADS = 4
RET_DIM = 128
RET_WIDTH = RET_HEADS * RET_DIM
D_FF = 4 * D_MODEL
PLE_DIM = 256
Q_BLOCK = 128
RET_CHUNK = 128
ROPE_BASE = 10000.0
EPS = 1e-6
IN_COLS = 3 * SB_WIDTH + 4 * RET_WIDTH + 2 * D_MODEL

kernel_name = "stickbreak_retention_hybrid_step"

F32 = jnp.float32


def rmsnorm(x, g):
    xf = x.astype(F32)
    y = xf * lax.rsqrt(jnp.mean(xf * xf, axis=-1, keepdims=True) + EPS)
    return (y * g.astype(F32)).astype(x.dtype)


def head_groupnorm(o, g):
    mu = jnp.mean(o, axis=-1, keepdims=True)
    var = jnp.mean(jnp.square(o - mu), axis=-1, keepdims=True)
    return (o - mu) * lax.rsqrt(var + EPS) * g.astype(F32)


def rope(x, pos):
    half = x.shape[-1] // 2
    freqs = ROPE_BASE ** (-jnp.arange(half, dtype=F32) / half)
    ang = pos.astype(F32)[:, None] * freqs[None, :]
    cos = jnp.cos(ang)[None, :, None, :]
    sin = jnp.sin(ang)[None, :, None, :]
    xf = x.astype(F32)
    x1, x2 = xf[..., :half], xf[..., half:]
    return jnp.concatenate([x1 * cos - x2 * sin, x1 * sin + x2 * cos], axis=-1)


def split_proj(proj):
    sizes = (SB_WIDTH, SB_WIDTH, SB_WIDTH, RET_WIDTH, RET_WIDTH, RET_WIDTH, RET_WIDTH, D_MODEL, D_MODEL)
    points = []
    acc = 0
    for s in sizes[:-1]:
        acc += s
        points.append(acc)
    return jnp.split(proj, points, axis=-1)


def stick_breaking(q, k, v, q_pos, k_pos, bias):
    z = (jnp.einsum("bqhd,bkhd->bhqk", q.astype(F32), k.astype(F32)) * (SB_DIM ** -0.5)
         + bias.astype(F32)[None, :, None, None])
    mask = (k_pos[None, :] < q_pos[:, None])[None, None]
    log_stay = jnp.where(mask, jax.nn.log_sigmoid(-z), 0.0)
    between = lax.cumsum(log_stay, axis=3, reverse=True) - log_stay
    a = jnp.where(mask, jnp.exp(jax.nn.log_sigmoid(z) + between), 0.0)
    return jnp.einsum("bhqk,bkhd->bqhd", a, v.astype(F32))


def retention_chunk(state, q, k, v, log_gamma):
    L = q.shape[1]
    idx = jnp.arange(L, dtype=F32)
    rel = idx[:, None] - idx[None, :]
    decay = jnp.where(rel[None] >= 0,
                      jnp.exp(jnp.maximum(rel, 0.0)[None] * log_gamma[:, None, None]), 0.0)
    scores = jnp.einsum("blhd,bmhd->bhlm", q, k) * decay[None]
    o_intra = jnp.einsum("bhlm,bmhe->blhe", scores, v)
    q_decay = jnp.exp((idx + 1.0)[:, None] * log_gamma[None, :])
    o_inter = jnp.einsum("blhd,bhde->blhe", q, state) * q_decay[None, :, :, None]
    k_decay = jnp.exp((L - 1.0 - idx)[:, None] * log_gamma[None, :])
    new_state = (jnp.exp(L * log_gamma)[None, :, None, None] * state
                 + jnp.einsum("blhd,blhe,lh->bhde", k, v, k_decay))
    return new_state, o_intra + o_inter


def hybrid_layer(x, p, pos, attend, retain, g_mix, w_in, sb_bias, w_sb_out, w_ret_out, w_o, ret_gn,
                 g_ffn, w_up, w_down, g_ple, w_ple_gate, w_ple):
    b, l, _ = x.shape
    u = rmsnorm(x, g_mix)
    q_a, k_a, v_a, q_r, k_r, v_r, g_r, gate_a, gate_r = split_proj(u @ w_in)
    q_a = q_a.reshape(b, l, SB_HEADS, SB_DIM)
    k_a = k_a.reshape(b, l, SB_HEADS, SB_DIM)
    v_a = v_a.reshape(b, l, SB_HEADS, SB_DIM)
    o_a = attend(q_a, k_a, v_a, sb_bias).astype(x.dtype)
    q_r = rope(q_r.reshape(b, l, RET_HEADS, RET_DIM), pos)
    k_r = rope(k_r.reshape(b, l, RET_HEADS, RET_DIM), pos) * (RET_DIM ** -0.5)
    v_r = v_r.reshape(b, l, RET_HEADS, RET_DIM).astype(F32)
    o_r, s_r = retain(q_r, k_r, v_r)
    o_r = head_groupnorm(o_r, ret_gn).reshape(b, l, RET_WIDTH).astype(x.dtype)
    y_a = o_a.reshape(b, l, SB_WIDTH) @ w_sb_out
    y_r = (jax.nn.silu(g_r) * o_r) @ w_ret_out
    merged = jax.nn.sigmoid(gate_a) * y_a + jax.nn.sigmoid(gate_r) * y_r
    h = x + merged @ w_o
    h = h + jnp.square(jax.nn.relu(rmsnorm(h, g_ffn) @ w_up)) @ w_down
    h = h + jax.nn.sigmoid(rmsnorm(h, g_ple) @ w_ple_gate) * (p @ w_ple)
    return h, k_a, v_a, s_r


def setup_inputs(seed: int = 0) -> dict:
    key = jax.random.key(seed)
    ks = jax.random.split(key, 24)
    n_pages = PAST_LEN // PAGE_SIZE
    n_pool = (DEC_BATCH * n_pages * 5) // 4

    def nrm(k, shape, scale=1.0):
        return jax.random.normal(k, shape, F32) * scale

    def gain(k, shape):
        return 1.0 + 0.01 * jax.random.normal(k, shape, F32)

    page_table = jax.random.permutation(ks[5], n_pool)[: DEC_BATCH * n_pages]
    page_table = page_table.reshape(DEC_BATCH, n_pages).astype(jnp.int32)
    sb_bias = -float(np.log(PAST_LEN)) + 0.1 * jax.random.normal(ks[21], (DEPTH, SB_HEADS), F32)
    return {
        "x_prompt": nrm(ks[0], (BATCH, SEQ, D_MODEL)),
        "x_sample": nrm(ks[1], (DEC_BATCH, DEC_SEQ, D_MODEL)),
        "cache_k": nrm(ks[2], (DEPTH, n_pool, PAGE_SIZE, SB_HEADS, SB_DIM)),
        "cache_v": nrm(ks[3], (DEPTH, n_pool, PAGE_SIZE, SB_HEADS, SB_DIM)),
        "state_ret": nrm(ks[4], (DEPTH, DEC_BATCH, RET_HEADS, RET_DIM, RET_DIM), 0.1),
        "page_table": page_table,
        "p_prompt": nrm(ks[6], (DEPTH, BATCH, SEQ, PLE_DIM)),
        "p_sample": nrm(ks[7], (DEPTH, DEC_BATCH, DEC_SEQ, PLE_DIM)),
        "g_mix": gain(ks[8], (DEPTH, D_MODEL)),
        "w_in": nrm(ks[9], (DEPTH, D_MODEL, IN_COLS), D_MODEL ** -0.5),
        "sb_bias": sb_bias,
        "w_sb_out": nrm(ks[10], (DEPTH, SB_WIDTH, D_MODEL), SB_WIDTH ** -0.5),
        "w_ret_out": nrm(ks[11], (DEPTH, RET_WIDTH, D_MODEL), RET_WIDTH ** -0.5),
        "w_o": nrm(ks[12], (DEPTH, D_MODEL, D_MODEL), D_MODEL ** -0.5),
        "ret_gn": gain(ks[13], (DEPTH, RET_HEADS, RET_DIM)),
        "g_ffn": gain(ks[14], (DEPTH, D_MODEL)),
        "w_up": nrm(ks[15], (DEPTH, D_MODEL, D_FF), D_MODEL ** -0.5),
        "w_down": nrm(ks[16], (DEPTH, D_FF, D_MODEL), D_FF ** -0.5),
        "g_ple": gain(ks[17], (DEPTH, D_MODEL)),
        "w_ple_gate": nrm(ks[18], (DEPTH, D_MODEL, D_MODEL), D_MODEL ** -0.5),
        "w_ple": nrm(ks[19], (DEPTH, PLE_DIM, D_MODEL), PLE_DIM ** -0.5),
        "g_final": gain(ks[20], (D_MODEL,)),
    }


def reference(x_prompt, x_sample, cache_k, cache_v, state_ret, page_table, p_prompt, p_sample,
              g_mix, w_in, sb_bias, w_sb_out, w_ret_out, w_o, ret_gn, g_ffn, w_up, w_down, g_ple,
              w_ple_gate, w_ple, g_final):
    log_gamma = jnp.log(1.0 - jnp.exp2(-5.0 - jnp.arange(RET_HEADS, dtype=F32)))
    seq = x_prompt.shape[1]
    dec_b, dec_l = x_sample.shape[0], x_sample.shape[1]
    n_pages = page_table.shape[1]
    past = n_pages * PAGE_SIZE
    pos_prompt = jnp.arange(seq, dtype=jnp.int32)
    pos_sample = past + jnp.arange(dec_l, dtype=jnp.int32)
    pos_keys_sample = jnp.arange(past + dec_l, dtype=jnp.int32)

    def attend_prompt(q, k, v, bias):
        b = q.shape[0]
        nb = seq // Q_BLOCK
        qb = q.reshape(b, nb, Q_BLOCK, SB_HEADS, SB_DIM).swapaxes(0, 1)
        pb = pos_prompt.reshape(nb, Q_BLOCK)
        o = lax.map(lambda a: stick_breaking(a[0], k, v, a[1], pos_prompt, bias), (qb, pb))
        return o.swapaxes(0, 1).reshape(b, seq, SB_HEADS, SB_DIM)

    def retain_prompt(q, k, v):
        b = q.shape[0]
        nc = seq // RET_CHUNK

        def to_chunks(t):
            return t.reshape(b, nc, RET_CHUNK, t.shape[2], t.shape[3]).swapaxes(0, 1)

        s0 = jnp.zeros((b, RET_HEADS, RET_DIM, RET_DIM), F32)

        def step(s, qkv):
            return retention_chunk(s, qkv[0], qkv[1], qkv[2], log_gamma)

        s_final, o = lax.scan(step, s0, (to_chunks(q), to_chunks(k), to_chunks(v)))
        return o.swapaxes(0, 1).reshape(b, seq, RET_HEADS, RET_DIM), s_final

    hp, hs = x_prompt, x_sample
    kp_list, vp_list, sp_list, ks_list, vs_list, ss_list = [], [], [], [], [], []
    for i in range(DEPTH):
        w = (g_mix[i], w_in[i], sb_bias[i], w_sb_out[i], w_ret_out[i], w_o[i], ret_gn[i], g_ffn[i],
             w_up[i], w_down[i], g_ple[i], w_ple_gate[i], w_ple[i])
        hp, kp, vp, sp = hybrid_layer(hp, p_prompt[i], pos_prompt, attend_prompt, retain_prompt, *w)

        past_k = cache_k[i][page_table].reshape(dec_b, past, SB_HEADS, SB_DIM)
        past_v = cache_v[i][page_table].reshape(dec_b, past, SB_HEADS, SB_DIM)

        def attend_sample(q, k, v, bias, past_k=past_k, past_v=past_v):
            k_all = jnp.concatenate([past_k.astype(k.dtype), k], axis=1)
            v_all = jnp.concatenate([past_v.astype(v.dtype), v], axis=1)
            return stick_breaking(q, k_all, v_all, pos_sample, pos_keys_sample, bias)

        def retain_sample(q, k, v, s_prev=state_ret[i]):
            s_new, o = retention_chunk(s_prev.astype(F32), q, k, v, log_gamma)
            return o, s_new

        hs, ks_, vs_, ss = hybrid_layer(hs, p_sample[i], pos_sample, attend_sample, retain_sample, *w)
        kp_list.append(kp)
        vp_list.append(vp)
        sp_list.append(sp.astype(x_prompt.dtype))
        ks_list.append(ks_)
        vs_list.append(vs_)
        ss_list.append(ss.astype(state_ret.dtype))

    y_prompt = rmsnorm(hp, g_final)
    y_sample = rmsnorm(hs, g_final)
    k_prompt = jnp.stack(kp_list)
    v_prompt = jnp.stack(vp_list)
    ret_state_prompt = jnp.stack(sp_list)
    k_sample = jnp.stack(ks_list)
    v_sample = jnp.stack(vs_list)
    ret_state_sample = jnp.stack(ss_list)
    return (y_prompt, y_sample, k_prompt, v_prompt, ret_state_prompt, k_sample, v_sample, ret_state_sample)
```

```python
import functools

import jax
import jax.numpy as jnp
import numpy as np
from jax import lax
from jax.experimental import pallas as pl
from jax.experimental.pallas import tpu as pltpu

F32 = jnp.float32
BF16 = jnp.bfloat16

D_MODEL = 1024
SB_HEADS = 8
SB_DIM = 64
SB_WIDTH = SB_HEADS * SB_DIM
RET_HEADS = 4
RET_DIM = 128
RET_WIDTH = RET_HEADS * RET_DIM
D_FF = 4 * D_MODEL
PAGE_SIZE = 128
ROPE_BASE = 10000.0
EPS = 1e-6
SB_SCALE = SB_DIM ** -0.5
RET_SCALE = RET_DIM ** -0.5

BRANCH_COLS = 512
COL_QA, COL_KA, COL_VA, COL_QR, COL_KR, COL_VR, COL_GR = range(7)
N_BRANCH_BLOCKS = 7
BRANCH_WIDTH = N_BRANCH_BLOCKS * BRANCH_COLS
GATE_WIDTH = 2 * D_MODEL

LANES = 128
VMEM_LIMIT_BYTES = 56 * 1024 * 1024

PROJ_ROWS = 512
POST_ROWS = 256
SB_BLOCK = 256
RET_CHUNK = 128
DEC_PAGES = 8
DEC_SEQS = 8


def _const_spec(shape):
    return pl.BlockSpec(shape, lambda *_: (0,) * len(shape), pipeline_mode=pl.Buffered(1))


def _rms(x, g):
    return x * lax.rsqrt(jnp.mean(x * x, axis=-1, keepdims=True) + EPS) * g


def _dot(a, b):
    return jnp.dot(a, b, preferred_element_type=F32)


def _dot_nt(a, b):
    return lax.dot_general(a, b, (((1,), (1,)), ((), ())), preferred_element_type=F32)


def _dot_tn(a, b):
    return lax.dot_general(a, b, (((0,), (0,)), ((), ())), preferred_element_type=F32)


def _proj_kernel(x_ref, g_ref, w_ref, branch_ref, gates_ref, k_ref, v_ref):
    u = _rms(x_ref[...], g_ref[...]).astype(BF16)
    for c in range(N_BRANCH_BLOCKS):
        cols = slice(c * BRANCH_COLS, (c + 1) * BRANCH_COLS)
        r = _dot(u, w_ref[:, cols])
        if c == COL_QA:
            r = r * SB_SCALE
        if c == COL_KA:
            k_ref[...] = r
        if c == COL_VA:
            v_ref[...] = r
        branch_ref[:, cols] = r.astype(branch_ref.dtype)
    for c in range(GATE_WIDTH // BRANCH_COLS):
        cols = slice(c * BRANCH_COLS, (c + 1) * BRANCH_COLS)
        wcols = slice(BRANCH_WIDTH + c * BRANCH_COLS, BRANCH_WIDTH + (c + 1) * BRANCH_COLS)
        gates_ref[:, cols] = _dot(u, w_ref[:, wcols]).astype(gates_ref.dtype)


def _in_projection(x, g_mix, w_in_bf16, act_dtype):
    t = x.shape[0]
    tm = PROJ_ROWS
    row = lambda i: (i, 0)
    return pl.pallas_call(
        _proj_kernel,
        grid=(t // tm,),
        in_specs=[
            pl.BlockSpec((tm, D_MODEL), row),
            _const_spec((1, D_MODEL)),
            _const_spec(w_in_bf16.shape),
        ],
        out_specs=[
            pl.BlockSpec((tm, BRANCH_WIDTH), row),
            pl.BlockSpec((tm, GATE_WIDTH), row),
            pl.BlockSpec((tm, SB_WIDTH), row),
            pl.BlockSpec((tm, SB_WIDTH), row),
        ],
        out_shape=[
            jax.ShapeDtypeStruct((t, BRANCH_WIDTH), act_dtype),
            jax.ShapeDtypeStruct((t, GATE_WIDTH), act_dtype),
            jax.ShapeDtypeStruct((t, SB_WIDTH), F32),
            jax.ShapeDtypeStruct((t, SB_WIDTH), F32),
        ],
        compiler_params=pltpu.CompilerParams(
            dimension_semantics=("parallel",), vmem_limit_bytes=VMEM_LIMIT_BYTES),
        name="in_projection",
    )(x, g_mix.reshape(1, D_MODEL), w_in_bf16)


def _stick_block(s, bias, tri, carry, mask):
    z = s + bias
    soft = jnp.log1p(jnp.exp(-jnp.abs(z)))
    log_stay = -jnp.maximum(z, 0.0) - soft
    log_beta = jnp.minimum(z, 0.0) - soft
    if mask is not None:
        log_stay = jnp.where(mask, log_stay, 0.0)
    right = _dot(log_stay.astype(BF16), tri)
    a = jnp.exp(log_beta + right + carry)
    if mask is not None:
        a = jnp.where(mask, a, 0.0)
    return a, carry + jnp.sum(log_stay, axis=1, keepdims=True)


def _sb_prompt_kernel(bias_ref, tri_ref, q_ref, k_ref, v_ref, o_ref):
    blk = SB_BLOCK
    i = pl.program_id(1)
    tri = tri_ref[...]
    row = lax.broadcasted_iota(jnp.int32, (blk, blk), 0)
    col = lax.broadcasted_iota(jnp.int32, (blk, blk), 1)
    causal = col < row
    lane = lax.broadcasted_iota(jnp.int32, (blk, LANES), 1)
    low_half = lane < SB_DIM

    for pair in range(SB_HEADS // 2):
        cols = slice(pair * LANES, (pair + 1) * LANES)
        q2 = q_ref[0, :, cols]
        accs = []
        for sub in range(2):
            q_head = jnp.where(low_half if sub == 0 else ~low_half, q2, jnp.zeros_like(q2))
            bias = bias_ref[2 * pair + sub]

            def step(kb, carry, acc, mask, q_head=q_head, bias=bias, cols=cols):
                start = pl.multiple_of(kb * blk, blk)
                k_blk = k_ref[0, pl.ds(start, blk), cols]
                v_blk = v_ref[0, pl.ds(start, blk), cols]
                a, carry = _stick_block(_dot_nt(q_head, k_blk), bias, tri, carry, mask)
                return carry, acc + _dot(a.astype(BF16), v_blk)

            carry = jnp.zeros((blk, 1), F32)
            acc = jnp.zeros((blk, LANES), F32)
            carry, acc = step(i, carry, acc, causal)
            carry, acc = lax.fori_loop(
                0, i, lambda t, c, step=step: step(i - 1 - t, c[0], c[1], None), (carry, acc))
            accs.append(acc)
        o_ref[0, :, cols] = jnp.where(low_half, accs[0], accs[1]).astype(o_ref.dtype)


def _tri(n):
    idx = jnp.arange(n)
    return (idx[:, None] > idx[None, :]).astype(BF16)


def _sb_prompt(branch, sb_bias, batch, seq):
    blk = SB_BLOCK
    return pl.pallas_call(
        _sb_prompt_kernel,
        grid=(batch, seq // blk),
        in_specs=[
            pl.BlockSpec(memory_space=pltpu.SMEM),
            _const_spec((blk, blk)),
            pl.BlockSpec((1, blk, SB_WIDTH), lambda b, i: (b, i, COL_QA)),
            pl.BlockSpec((1, seq, SB_WIDTH), lambda b, i: (b, 0, COL_KA)),
            pl.BlockSpec((1, seq, SB_WIDTH), lambda b, i: (b, 0, COL_VA)),
        ],
        out_specs=pl.BlockSpec((1, blk, SB_WIDTH), lambda b, i: (b, i, 0)),
        out_shape=jax.ShapeDtypeStruct((batch, seq, SB_WIDTH), BF16),
        compiler_params=pltpu.CompilerParams(
            dimension_semantics=("parallel", "arbitrary"), vmem_limit_bytes=VMEM_LIMIT_BYTES),
        name="sb_prompt",
    )(sb_bias, _tri(blk), branch, branch, branch)


def _sb_decode_kernel(pt_ref, bias_ref, tri_ref, q_ref, kn_ref, vn_ref, *rest):
    del pt_ref
    k_pages = rest[:DEC_PAGES]
    v_pages = rest[DEC_PAGES:2 * DEC_PAGES]
    o_ref, qbd_ref, kpad_ref, vpad_ref, acc_ref, carry_ref = rest[2 * DEC_PAGES:]
    j = pl.program_id(1)
    dec_l = q_ref.shape[1]
    rows = SB_HEADS * dec_l
    tri = tri_ref[...]

    row_head = lax.broadcasted_iota(jnp.int32, (rows, 1), 0) // dec_l
    bias = jnp.zeros((rows, 1), F32)
    for h in range(SB_HEADS):
        bias = jnp.where(row_head == h, bias_ref[h], bias)

    def fold(k_blk, v_blk, tri_blk, mask):
        a, carry = _stick_block(_dot_nt(qbd_ref[...], k_blk), bias, tri_blk, carry_ref[...], mask)
        carry_ref[...] = carry
        acc_ref[...] += _dot(a.astype(BF16), v_blk)

    @pl.when(j == 0)
    def _():
        q = q_ref[0]
        q_rep = jnp.concatenate([q] * SB_HEADS, axis=0)
        r_head = lax.broadcasted_iota(jnp.int32, (rows, SB_WIDTH), 0) // dec_l
        c_head = lax.broadcasted_iota(jnp.int32, (rows, SB_WIDTH), 1) // SB_DIM
        qbd_ref[...] = jnp.where(r_head == c_head, q_rep, 0.0).astype(BF16)
        acc_ref[...] = jnp.zeros_like(acc_ref)
        carry_ref[...] = jnp.zeros_like(carry_ref)
        kpad_ref[...] = jnp.zeros_like(kpad_ref)
        vpad_ref[...] = jnp.zeros_like(vpad_ref)
        kpad_ref[0:dec_l, :] = kn_ref[0].astype(BF16)
        vpad_ref[0:dec_l, :] = vn_ref[0].astype(BF16)
        r_query = lax.broadcasted_iota(jnp.int32, (rows, PAGE_SIZE), 0) % dec_l
        c_key = lax.broadcasted_iota(jnp.int32, (rows, PAGE_SIZE), 1)
        fold(kpad_ref[...], vpad_ref[...], tri[:PAGE_SIZE, :PAGE_SIZE], c_key < r_query)

    for m in reversed(range(DEC_PAGES // 2)):
        k2 = jnp.concatenate([k_pages[2 * m][0], k_pages[2 * m + 1][0]], axis=0).astype(BF16)
        v2 = jnp.concatenate([v_pages[2 * m][0], v_pages[2 * m + 1][0]], axis=0).astype(BF16)
        fold(k2, v2, tri, None)

    @pl.when(j == pl.num_programs(1) - 1)
    def _():
        r_head = lax.broadcasted_iota(jnp.int32, (rows, SB_WIDTH), 0) // dec_l
        c_head = lax.broadcasted_iota(jnp.int32, (rows, SB_WIDTH), 1) // SB_DIM
        own = jnp.where(r_head == c_head, acc_ref[...], 0.0)
        out = own[0:dec_l]
        for h in range(1, SB_HEADS):
            out = out + own[h * dec_l:(h + 1) * dec_l]
        o_ref[0] = out


def _sb_decode(branch, cache_k, cache_v, page_table, sb_bias):
    dec_b, dec_l, _ = branch.shape
    n_pages = page_table.shape[1]
    n_steps = n_pages // DEC_PAGES
    rows = SB_HEADS * dec_l

    def page_spec(p):
        def index(b, j, pt):
            return (pt[b, (n_steps - 1 - j) * DEC_PAGES + p], 0, 0)
        return pl.BlockSpec((1, PAGE_SIZE, SB_WIDTH), index)

    tri_n = 2 * PAGE_SIZE
    grid_spec = pltpu.PrefetchScalarGridSpec(
        num_scalar_prefetch=1,
        grid=(dec_b, n_steps),
        in_specs=[
            pl.BlockSpec(memory_space=pltpu.SMEM),
            pl.BlockSpec((tri_n, tri_n), lambda b, j, pt: (0, 0), pipeline_mode=pl.Buffered(1)),
            pl.BlockSpec((1, dec_l, SB_WIDTH), lambda b, j, pt: (b, 0, COL_QA)),
            pl.BlockSpec((1, dec_l, SB_WIDTH), lambda b, j, pt: (b, 0, COL_KA)),
            pl.BlockSpec((1, dec_l, SB_WIDTH), lambda b, j, pt: (b, 0, COL_VA)),
        ] + [page_spec(p) for p in range(DEC_PAGES)] * 2,
        out_specs=pl.BlockSpec((1, dec_l, SB_WIDTH), lambda b, j, pt: (b, 0, 0)),
        scratch_shapes=[
            pltpu.VMEM((rows, SB_WIDTH), BF16),
            pltpu.VMEM((PAGE_SIZE, SB_WIDTH), BF16),
            pltpu.VMEM((PAGE_SIZE, SB_WIDTH), BF16),
            pltpu.VMEM((rows, SB_WIDTH), F32),
            pltpu.VMEM((rows, 1), F32),
        ],
    )
    return pl.pallas_call(
        _sb_decode_kernel,
        grid_spec=grid_spec,
        out_shape=jax.ShapeDtypeStruct((dec_b, dec_l, SB_WIDTH), F32),
        compiler_params=pltpu.CompilerParams(
            dimension_semantics=("parallel", "arbitrary"), vmem_limit_bytes=VMEM_LIMIT_BYTES),
        name="sb_decode",
    )(page_table, sb_bias, _tri(tri_n), branch, branch, branch,
      *([cache_k] * DEC_PAGES), *([cache_v] * DEC_PAGES))


def _log_gamma():
    return jnp.log(1.0 - jnp.exp2(-5.0 - jnp.arange(RET_HEADS, dtype=F32)))


def _rope_tables(pos):
    half = RET_DIM // 2
    freqs = ROPE_BASE ** (-jnp.arange(half, dtype=F32) / half)
    ang = pos.astype(F32)[:, None] * freqs[None, :]
    cos, sin = jnp.cos(ang), jnp.sin(ang)
    return jnp.concatenate([cos, cos], axis=1), jnp.concatenate([-sin, sin], axis=1)


def _decay_tables(length):
    lg = _log_gamma()
    idx = jnp.arange(length, dtype=F32)
    rel = idx[:, None] - idx[None, :]
    intra = jnp.where(rel[None] >= 0, jnp.exp(jnp.maximum(rel, 0.0)[None] * lg[:, None, None]), 0.0)
    q_decay = jnp.exp((idx + 1.0)[None, :] * lg[:, None])
    k_decay = jnp.exp((length - 1.0 - idx)[None, :] * lg[:, None])
    rep = lambda t: jnp.broadcast_to(t[:, :, None], (RET_HEADS, length, LANES))
    return intra, rep(q_decay), rep(k_decay), jnp.exp(length * lg)


def _rope(x, cos2, sin2):
    return x * cos2 + pltpu.roll(x, RET_DIM // 2, axis=1) * sin2


def _groupnorm_gate(o, gn, g):
    mu = jnp.mean(o, axis=-1, keepdims=True)
    d = o - mu
    var = jnp.mean(d * d, axis=-1, keepdims=True)
    return jax.nn.silu(g) * (d * lax.rsqrt(var + EPS) * gn)


def _ret_prompt_kernel(sdec_ref, cos_ref, sin_ref, intra_ref, qdec_ref, kdec_ref, gn_ref,
                       q_ref, k_ref, v_ref, g_ref, r_ref, sfin_ref, state_ref):
    c = pl.program_id(1)

    @pl.when(c == 0)
    def _():
        state_ref[...] = jnp.zeros_like(state_ref)

    cos2, sin2 = cos_ref[...], sin_ref[...]
    for h in range(RET_HEADS):
        cols = slice(h * RET_DIM, (h + 1) * RET_DIM)
        q = _rope(q_ref[0, :, cols].astype(F32), cos2, sin2)
        k = _rope(k_ref[0, :, cols].astype(F32), cos2, sin2) * RET_SCALE
        v = v_ref[0, :, cols]
        qb = q.astype(BF16)
        state = state_ref[h]
        scores = _dot_nt(qb, k.astype(BF16)) * intra_ref[h]
        o = _dot(scores.astype(BF16), v) + _dot(qb, state.astype(BF16)) * qdec_ref[h]
        state_ref[h] = sdec_ref[h] * state + _dot_tn((k * kdec_ref[h]).astype(BF16), v)
        r_ref[0, :, cols] = _groupnorm_gate(
            o, gn_ref[h], g_ref[0, :, cols].astype(F32)).astype(r_ref.dtype)

    @pl.when(c == pl.num_programs(1) - 1)
    def _():
        sfin_ref[0] = state_ref[...]


def _ret_prompt(branch, ret_gn, batch, seq):
    ch = RET_CHUNK
    cos2, sin2 = _rope_tables(jnp.arange(seq, dtype=jnp.int32))
    intra, q_decay, k_decay, s_decay = _decay_tables(ch)
    col = lambda cb: (lambda b, c: (b, c, cb))
    blk = (1, ch, RET_WIDTH)
    return pl.pallas_call(
        _ret_prompt_kernel,
        grid=(batch, seq // ch),
        in_specs=[
            pl.BlockSpec(memory_space=pltpu.SMEM),
            pl.BlockSpec((ch, RET_DIM), lambda b, c: (c, 0)),
            pl.BlockSpec((ch, RET_DIM), lambda b, c: (c, 0)),
            _const_spec((RET_HEADS, ch, ch)),
            _const_spec((RET_HEADS, ch, LANES)),
            _const_spec((RET_HEADS, ch, LANES)),
            _const_spec((RET_HEADS, 1, RET_DIM)),
            pl.BlockSpec(blk, col(COL_QR)),
            pl.BlockSpec(blk, col(COL_KR)),
            pl.BlockSpec(blk, col(COL_VR)),
            pl.BlockSpec(blk, col(COL_GR)),
        ],
        out_specs=[
            pl.BlockSpec(blk, lambda b, c: (b, c, 0)),
            pl.BlockSpec((1, RET_HEADS, RET_DIM, RET_DIM), lambda b, c: (b, 0, 0, 0)),
        ],
        out_shape=[
            jax.ShapeDtypeStruct((batch, seq, RET_WIDTH), BF16),
            jax.ShapeDtypeStruct((batch, RET_HEADS, RET_DIM, RET_DIM), F32),
        ],
        scratch_shapes=[pltpu.VMEM((RET_HEADS, RET_DIM, RET_DIM), F32)],
        compiler_params=pltpu.CompilerParams(
            dimension_semantics=("parallel", "arbitrary"), vmem_limit_bytes=VMEM_LIMIT_BYTES),
        name="ret_prompt",
    )(s_decay, cos2, sin2, intra, q_decay, k_decay, ret_gn.reshape(RET_HEADS, 1, RET_DIM),
      branch, branch, branch, branch)


def _ret_decode_kernel(sdec_ref, cos_ref, sin_ref, intra_ref, qdec_ref, kdec_ref, gn_ref,
                       q_ref, k_ref, v_ref, g_ref, s_ref, r_ref, snew_ref):
    n_seq, dec_l, _ = q_ref.shape
    cos2, sin2 = cos_ref[...], sin_ref[...]

    def one_seq(s, _):
        for h in range(RET_HEADS):
            cols = slice(h * RET_DIM, (h + 1) * RET_DIM)
            q = _rope(q_ref[s, :, cols], cos2, sin2)
            k = _rope(k_ref[s, :, cols], cos2, sin2) * RET_SCALE
            v = v_ref[s, :, cols].astype(BF16)
            qb = q.astype(BF16)
            state = s_ref[s, h]
            scores = _dot_nt(qb, k.astype(BF16)) * intra_ref[h]
            o = _dot(scores.astype(BF16), v) + _dot(qb, state.astype(BF16)) * qdec_ref[h]
            snew_ref[s, h] = sdec_ref[h] * state + _dot_tn((k * kdec_ref[h]).astype(BF16), v)
            r_ref[s, :, cols] = _groupnorm_gate(o, gn_ref[h], g_ref[s, :, cols])
        return 0

    lax.fori_loop(0, n_seq, one_seq, 0)


def _ret_decode(branch, state, ret_gn, past):
    dec_b, dec_l, _ = branch.shape
    g = DEC_SEQS
    cos2, sin2 = _rope_tables(past + jnp.arange(dec_l, dtype=jnp.int32))
    intra, q_decay, k_decay, s_decay = _decay_tables(dec_l)
    col = lambda cb: (lambda i: (i, 0, cb))
    blk = (g, dec_l, RET_WIDTH)
    sblk = (g, RET_HEADS, RET_DIM, RET_DIM)
    return pl.pallas_call(
        _ret_decode_kernel,
        grid=(dec_b // g,),
        in_specs=[
            pl.BlockSpec(memory_space=pltpu.SMEM),
            _const_spec((dec_l, RET_DIM)),
            _const_spec((dec_l, RET_DIM)),
            _const_spec((RET_HEADS, dec_l, dec_l)),
            _const_spec((RET_HEADS, dec_l, LANES)),
            _const_spec((RET_HEADS, dec_l, LANES)),
            _const_spec((RET_HEADS, 1, RET_DIM)),
            pl.BlockSpec(blk, col(COL_QR)),
            pl.BlockSpec(blk, col(COL_KR)),
            pl.BlockSpec(blk, col(COL_VR)),
            pl.BlockSpec(blk, col(COL_GR)),
            pl.BlockSpec(sblk, lambda i: (i, 0, 0, 0)),
        ],
        out_specs=[
            pl.BlockSpec(blk, lambda i: (i, 0, 0)),
            pl.BlockSpec(sblk, lambda i: (i, 0, 0, 0)),
        ],
        out_shape=[
            jax.ShapeDtypeStruct((dec_b, dec_l, RET_WIDTH), F32),
            jax.ShapeDtypeStruct(state.shape, state.dtype),
        ],
        compiler_params=pltpu.CompilerParams(
            dimension_semantics=("parallel",), vmem_limit_bytes=VMEM_LIMIT_BYTES),
        name="ret_decode",
    )(s_decay, cos2, sin2, intra, q_decay, k_decay, ret_gn.reshape(RET_HEADS, 1, RET_DIM),
      branch, branch, branch, branch, state)


def _post_kernel(oa_ref, r_ref, gates_ref, x_ref, p_ref, wsb_ref, wret_ref, wo_ref, gffn_ref,
                 wup_ref, wdown_ref, gple_ref, wpg_ref, wple_ref, gfin_ref, y_ref, *, final_norm):
    y_a = _dot(oa_ref[...].astype(BF16), wsb_ref[...])
    y_r = _dot(r_ref[...].astype(BF16), wret_ref[...])
    gate_a = gates_ref[:, :D_MODEL].astype(F32)
    gate_r = gates_ref[:, D_MODEL:].astype(F32)
    merged = jax.nn.sigmoid(gate_a) * y_a + jax.nn.sigmoid(gate_r) * y_r
    h = x_ref[...] + _dot(merged.astype(BF16), wo_ref[...])

    u = _rms(h, gffn_ref[...]).astype(BF16)
    for c in range(D_FF // D_MODEL):
        cols = slice(c * D_MODEL, (c + 1) * D_MODEL)
        act = jnp.square(jnp.maximum(_dot(u, wup_ref[:, cols]), 0.0))
        h = h + _dot(act.astype(BF16), wdown_ref[cols, :])

    u = _rms(h, gple_ref[...]).astype(BF16)
    gate = jax.nn.sigmoid(_dot(u, wpg_ref[...]))
    h = h + gate * _dot(p_ref[...].astype(BF16), wple_ref[...])
    y_ref[...] = _rms(h, gfin_ref[...]) if final_norm else h


def _post(o_a, r, gates, x, p, w, g_final, final_norm):
    t = x.shape[0]
    tm = POST_ROWS
    row = lambda i: (i, 0)
    weights = [w["w_sb_out"], w["w_ret_out"], w["w_o"], w["g_ffn"], w["w_up"], w["w_down"],
               w["g_ple"], w["w_ple_gate"], w["w_ple"], g_final]
    acts = [o_a, r, gates, x, p]
    return pl.pallas_call(
        functools.partial(_post_kernel, final_norm=final_norm),
        grid=(t // tm,),
        in_specs=[pl.BlockSpec((tm, a.shape[1]), row) for a in acts]
        + [_const_spec(a.shape) for a in weights],
        out_specs=pl.BlockSpec((tm, D_MODEL), row),
        out_shape=jax.ShapeDtypeStruct((t, D_MODEL), F32),
        compiler_params=pltpu.CompilerParams(
            dimension_semantics=("parallel",), vmem_limit_bytes=VMEM_LIMIT_BYTES),
        name="post",
    )(*acts, *weights)


def kernel(x_prompt, x_sample, cache_k, cache_v, state_ret, page_table, p_prompt, p_sample,
           g_mix, w_in, sb_bias, w_sb_out, w_ret_out, w_o, ret_gn, g_ffn, w_up, w_down, g_ple,
           w_ple_gate, w_ple, g_final):
    depth = g_mix.shape[0]
    batch, seq, _ = x_prompt.shape
    dec_b, dec_l, _ = x_sample.shape
    n_pool = cache_k.shape[1]
    past = page_table.shape[1] * PAGE_SIZE
    gain = lambda g: g.reshape(1, D_MODEL)

    hp = x_prompt.reshape(batch * seq, D_MODEL)
    hs = x_sample.reshape(dec_b * dec_l, D_MODEL)
    outs = {name: [] for name in ("kp", "vp", "sp", "ks", "vs", "ss")}
    for i in range(depth):
        last = i == depth - 1
        w_in_i = w_in[i].astype(BF16)
        w = {
            "w_sb_out": w_sb_out[i].astype(BF16), "w_ret_out": w_ret_out[i].astype(BF16),
            "w_o": w_o[i].astype(BF16), "g_ffn": gain(g_ffn[i]), "w_up": w_up[i].astype(BF16),
            "w_down": w_down[i].astype(BF16), "g_ple": gain(g_ple[i]),
            "w_ple_gate": w_ple_gate[i].astype(BF16), "w_ple": w_ple[i].astype(BF16),
        }

        branch, gates, k_a, v_a = _in_projection(hp, g_mix[i], w_in_i, BF16)
        branch3 = branch.reshape(batch, seq, BRANCH_WIDTH)
        o_a = _sb_prompt(branch3, sb_bias[i], batch, seq)
        r, s_fin = _ret_prompt(branch3, ret_gn[i], batch, seq)
        hp = _post(o_a.reshape(batch * seq, SB_WIDTH), r.reshape(batch * seq, RET_WIDTH), gates, hp,
                   p_prompt[i].reshape(batch * seq, -1), w, gain(g_final), last)
        outs["kp"].append(k_a.reshape(batch, seq, SB_HEADS, SB_DIM))
        outs["vp"].append(v_a.reshape(batch, seq, SB_HEADS, SB_DIM))
        outs["sp"].append(s_fin)

        branch, gates, k_a, v_a = _in_projection(hs, g_mix[i], w_in_i, F32)
        branch3 = branch.reshape(dec_b, dec_l, BRANCH_WIDTH)
        o_a = _sb_decode(branch3, cache_k[i].reshape(n_pool, PAGE_SIZE, SB_WIDTH),
                         cache_v[i].reshape(n_pool, PAGE_SIZE, SB_WIDTH), page_table, sb_bias[i])
        r, s_new = _ret_decode(branch3, state_ret[i], ret_gn[i], past)
        hs = _post(o_a.reshape(dec_b * dec_l, SB_WIDTH), r.reshape(dec_b * dec_l, RET_WIDTH), gates,
                   hs, p_sample[i].reshape(dec_b * dec_l, -1), w, gain(g_final), last)
        outs["ks"].append(k_a.reshape(dec_b, dec_l, SB_HEADS, SB_DIM))
        outs["vs"].append(v_a.reshape(dec_b, dec_l, SB_HEADS, SB_DIM))
        outs["ss"].append(s_new)

    return (hp.reshape(batch, seq, D_MODEL), hs.reshape(dec_b, dec_l, D_MODEL),
            jnp.stack(outs["kp"]), jnp.stack(outs["vp"]), jnp.stack(outs["sp"]),
            jnp.stack(outs["ks"]), jnp.stack(outs["vs"]), jnp.stack(outs["ss"]))
```

```python
import functools

import jax
import jax.numpy as jnp
import numpy as np
from jax import lax
from jax.experimental import pallas as pl
from jax.experimental.pallas import tpu as pltpu

F32 = jnp.float32
BF16 = jnp.bfloat16

D_MODEL = 1024
SB_HEADS = 8
SB_DIM = 64
SB_WIDTH = SB_HEADS * SB_DIM
RET_HEADS = 4
RET_DIM = 128
RET_WIDTH = RET_HEADS * RET_DIM
D_FF = 4 * D_MODEL
PAGE_SIZE = 128
ROPE_BASE = 10000.0
EPS = 1e-6
SB_SCALE = SB_DIM ** -0.5
RET_SCALE = RET_DIM ** -0.5

BRANCH_COLS = 512
COL_QA, COL_KA, COL_VA, COL_QR, COL_KR, COL_VR, COL_GR = range(7)
N_BRANCH_BLOCKS = 7
BRANCH_WIDTH = N_BRANCH_BLOCKS * BRANCH_COLS
GATE_WIDTH = 2 * D_MODEL

LANES = 128
VMEM_LIMIT_BYTES = 56 * 1024 * 1024

PROJ_ROWS = 512
POST_ROWS = 256
SB_BLOCK = 256
RET_CHUNK = 128
DEC_PAGES = 16
DEC_SEQS = 8


def _const_spec(shape):
    return pl.BlockSpec(shape, lambda *_: (0,) * len(shape), pipeline_mode=pl.Buffered(1))


def _rms(x, g):
    return x * lax.rsqrt(jnp.mean(x * x, axis=-1, keepdims=True) + EPS) * g


def _dot(a, b):
    return jnp.dot(a, b, preferred_element_type=F32)


def _dot_nt(a, b):
    return lax.dot_general(a, b, (((1,), (1,)), ((), ())), preferred_element_type=F32)


def _dot_tn(a, b):
    return lax.dot_general(a, b, (((0,), (0,)), ((), ())), preferred_element_type=F32)


def _proj_kernel(x_ref, g_ref, w_ref, branch_ref, gates_ref, k_ref, v_ref):
    u = _rms(x_ref[...], g_ref[...]).astype(BF16)
    for c in range(N_BRANCH_BLOCKS):
        cols = slice(c * BRANCH_COLS, (c + 1) * BRANCH_COLS)
        r = _dot(u, w_ref[:, cols])
        if c == COL_QA:
            r = r * SB_SCALE
        if c == COL_KA:
            k_ref[...] = r
        if c == COL_VA:
            v_ref[...] = r
        branch_ref[:, cols] = r.astype(branch_ref.dtype)
    for c in range(GATE_WIDTH // BRANCH_COLS):
        cols = slice(c * BRANCH_COLS, (c + 1) * BRANCH_COLS)
        wcols = slice(BRANCH_WIDTH + c * BRANCH_COLS, BRANCH_WIDTH + (c + 1) * BRANCH_COLS)
        gates_ref[:, cols] = _dot(u, w_ref[:, wcols]).astype(gates_ref.dtype)


def _in_projection(x, g_mix, w_in_bf16, act_dtype):
    t = x.shape[0]
    tm = PROJ_ROWS
    row = lambda i: (i, 0)
    return pl.pallas_call(
        _proj_kernel,
        grid=(t // tm,),
        in_specs=[
            pl.BlockSpec((tm, D_MODEL), row),
            _const_spec((1, D_MODEL)),
            _const_spec(w_in_bf16.shape),
        ],
        out_specs=[
            pl.BlockSpec((tm, BRANCH_WIDTH), row),
            pl.BlockSpec((tm, GATE_WIDTH), row),
            pl.BlockSpec((tm, SB_WIDTH), row),
            pl.BlockSpec((tm, SB_WIDTH), row),
        ],
        out_shape=[
            jax.ShapeDtypeStruct((t, BRANCH_WIDTH), act_dtype),
            jax.ShapeDtypeStruct((t, GATE_WIDTH), act_dtype),
            jax.ShapeDtypeStruct((t, SB_WIDTH), F32),
            jax.ShapeDtypeStruct((t, SB_WIDTH), F32),
        ],
        compiler_params=pltpu.CompilerParams(
            dimension_semantics=("parallel",), vmem_limit_bytes=VMEM_LIMIT_BYTES),
        name="in_projection",
    )(x, g_mix.reshape(1, D_MODEL), w_in_bf16)


def _stick_block(z, tri, carry, mask):
    m, n = z.shape
    stay = jnp.maximum(z, 0.0) + jnp.log(1.0 + jnp.exp(-jnp.abs(z)))
    if mask is not None:
        stay = jnp.where(mask, stay, 0.0)
    right = _dot(stay.astype(BF16), tri)
    carry_n = jnp.concatenate([carry] * (n // LANES), axis=1)
    a = jnp.exp(z - stay - right - carry_n)
    if mask is not None:
        a = jnp.where(mask, a, 0.0)
    return a, jnp.broadcast_to(jnp.sum(stay, axis=1, keepdims=True), (m, LANES))


def _sb_prompt_kernel(bias_ref, tri_ref, q_ref, k_ref, v_ref, o_ref, acc_ref, carry_ref):
    blk = SB_BLOCK
    i = pl.program_id(1)
    row = lax.broadcasted_iota(jnp.int32, (blk, blk), 0)
    col = lax.broadcasted_iota(jnp.int32, (blk, blk), 1)
    causal = col < row
    low_half = lax.broadcasted_iota(jnp.int32, (blk, LANES), 1) < SB_DIM
    acc_ref[...] = jnp.zeros_like(acc_ref)
    carry_ref[...] = jnp.zeros_like(carry_ref)

    def key_block(kb, mask):
        start = pl.multiple_of(kb * blk, blk)
        tri = tri_ref[...]
        for pair in range(SB_HEADS // 2):
            cols = slice(pair * LANES, (pair + 1) * LANES)
            q2 = q_ref[0, :, cols]
            k2 = k_ref[0, pl.ds(start, blk), cols]
            v2 = v_ref[0, pl.ds(start, blk), cols]
            update = None
            for sub in range(2):
                head = 2 * pair + sub
                own = low_half if sub == 0 else ~low_half
                q_head = jnp.where(own, q2, jnp.zeros_like(q2))
                v_head = jnp.where(own, v2, jnp.zeros_like(v2))
                z = _dot_nt(q_head, k2) + bias_ref[head]
                a, stay_sum = _stick_block(z, tri, carry_ref[head], mask)
                carry_ref[head] += stay_sum
                d = _dot(a.astype(BF16), v_head)
                update = d if update is None else update + d
            acc_ref[pair] += update

    key_block(i, causal)

    def body(t, c):
        key_block(i - 1 - t, None)
        return c

    lax.fori_loop(0, i, body, 0)
    for pair in range(SB_HEADS // 2):
        o_ref[0, :, pair * LANES:(pair + 1) * LANES] = acc_ref[pair].astype(o_ref.dtype)


def _tri(n):
    idx = jnp.arange(n)
    return (idx[:, None] > idx[None, :]).astype(BF16)


def _sb_prompt(branch, sb_bias, batch, seq):
    blk = SB_BLOCK
    return pl.pallas_call(
        _sb_prompt_kernel,
        grid=(batch, seq // blk),
        in_specs=[
            pl.BlockSpec(memory_space=pltpu.SMEM),
            _const_spec((blk, blk)),
            pl.BlockSpec((1, blk, SB_WIDTH), lambda b, i: (b, i, COL_QA)),
            pl.BlockSpec((1, seq, SB_WIDTH), lambda b, i: (b, 0, COL_KA)),
            pl.BlockSpec((1, seq, SB_WIDTH), lambda b, i: (b, 0, COL_VA)),
        ],
        out_specs=pl.BlockSpec((1, blk, SB_WIDTH), lambda b, i: (b, i, 0)),
        out_shape=jax.ShapeDtypeStruct((batch, seq, SB_WIDTH), BF16),
        scratch_shapes=[
            pltpu.VMEM((SB_HEADS // 2, blk, LANES), F32),
            pltpu.VMEM((SB_HEADS, blk, LANES), F32),
        ],
        compiler_params=pltpu.CompilerParams(
            dimension_semantics=("parallel", "arbitrary"), vmem_limit_bytes=VMEM_LIMIT_BYTES),
        name="sb_prompt",
    )(sb_bias, _tri(blk), branch, branch, branch)


def _sb_decode_kernel(pt_ref, page_bias_ref, new_bias_ref, tri_ref, q_ref, kn_ref, vn_ref, *rest):
    del pt_ref
    k_pages = rest[:DEC_PAGES]
    v_pages = rest[DEC_PAGES:2 * DEC_PAGES]
    o_ref, qrow_ref, qbd_ref, kpad_ref, vpad_ref, onew_ref, ocache_ref, carry_ref = rest[2 * DEC_PAGES:]
    j = pl.program_id(1)
    dec_l = q_ref.shape[1]
    rows = SB_HEADS * dec_l
    page_keys = PAGE_SIZE * SB_HEADS
    tri_n = tri_ref.shape[0]

    @pl.when(j == 0)
    def _():
        q = q_ref[0]
        qrow_ref[...] = jnp.concatenate(
            [q[:, h * SB_DIM:(h + 1) * SB_DIM] for h in range(SB_HEADS)], axis=0)
        r_head = lax.broadcasted_iota(jnp.int32, (rows, SB_WIDTH), 0) // dec_l
        c_head = lax.broadcasted_iota(jnp.int32, (rows, SB_WIDTH), 1) // SB_DIM
        own = r_head == c_head
        qbd_ref[...] = jnp.where(own, jnp.concatenate([q] * SB_HEADS, axis=0), 0.0).astype(BF16)
        kpad_ref[...] = jnp.zeros_like(kpad_ref)
        vpad_ref[...] = jnp.zeros_like(vpad_ref)
        kpad_ref[0:dec_l, :] = kn_ref[0].astype(BF16)
        vpad_ref[0:dec_l, :] = vn_ref[0].astype(BF16)
        z = _dot_nt(qbd_ref[...], kpad_ref[...]) + new_bias_ref[...]
        a, stay_sum = _stick_block(z, tri_ref[:PAGE_SIZE, :PAGE_SIZE], jnp.zeros((rows, LANES), F32), None)
        carry_ref[...] = stay_sum
        wide = jnp.where(own, _dot(a.astype(BF16), vpad_ref[...]), 0.0)
        out = wide[0:dec_l]
        for h in range(1, SB_HEADS):
            out = out + wide[h * dec_l:(h + 1) * dec_l]
        onew_ref[...] = out
        ocache_ref[...] = jnp.zeros_like(ocache_ref)

    n_chunks = page_keys // tri_n
    carry = carry_ref[...]
    o_cache = ocache_ref[...]
    for p in reversed(range(DEC_PAGES)):
        k_flat = k_pages[p][0, 0].reshape(page_keys, SB_DIM)
        v_flat = v_pages[p][0, 0].reshape(page_keys, SB_DIM)
        z = _dot_nt(qrow_ref[...], k_flat) + page_bias_ref[...]
        stay = jnp.maximum(z, 0.0) + jnp.log(1.0 + jnp.exp(-jnp.abs(z)))
        chunks = [stay[:, c * tri_n:(c + 1) * tri_n] for c in range(n_chunks)]
        right = _dot(jnp.concatenate(chunks, axis=0).astype(BF16), tri_ref[...])
        weights = [None] * n_chunks
        for c in reversed(range(n_chunks)):
            carry_n = jnp.concatenate([carry] * (tri_n // LANES), axis=1)
            weights[c] = jnp.exp(z[:, c * tri_n:(c + 1) * tri_n] - chunks[c]
                                 - right[c * rows:(c + 1) * rows] - carry_n)
            carry = carry + jnp.broadcast_to(
                jnp.sum(chunks[c], axis=1, keepdims=True), (rows, LANES))
        o_cache = o_cache + _dot(jnp.concatenate(weights, axis=1), v_flat)
    carry_ref[...] = carry
    ocache_ref[...] = o_cache

    @pl.when(j == pl.num_programs(1) - 1)
    def _():
        oc = ocache_ref[...]
        o_ref[0] = onew_ref[...] + jnp.concatenate(
            [oc[h * dec_l:(h + 1) * dec_l] for h in range(SB_HEADS)], axis=1)


def _sb_decode(branch, cache_k, cache_v, layer, page_table, sb_bias):
    dec_b, dec_l, _ = branch.shape
    n_pages = page_table.shape[1]
    n_steps = n_pages // DEC_PAGES
    rows = SB_HEADS * dec_l
    page_keys = PAGE_SIZE * SB_HEADS
    tri_n = 2 * LANES

    neg = jnp.float32(-1e30)
    row_head = jnp.arange(rows) // dec_l
    row_query = jnp.arange(rows) % dec_l
    page_bias = jnp.where(row_head[:, None] == (jnp.arange(page_keys) % SB_HEADS)[None, :],
                          sb_bias[row_head][:, None], neg)
    new_bias = jnp.where(jnp.arange(PAGE_SIZE)[None, :] < row_query[:, None],
                         sb_bias[row_head][:, None], neg)

    def page_spec(p):
        def index(b, j, pt):
            return (layer, pt[b, (n_steps - 1 - j) * DEC_PAGES + p], 0, 0, 0)
        return pl.BlockSpec((1, 1, PAGE_SIZE, SB_HEADS, SB_DIM), index)

    const = lambda shape: pl.BlockSpec(shape, lambda b, j, pt: (0,) * len(shape),
                                       pipeline_mode=pl.Buffered(1))
    grid_spec = pltpu.PrefetchScalarGridSpec(
        num_scalar_prefetch=1,
        grid=(dec_b, n_steps),
        in_specs=[
            const((rows, page_keys)),
            const((rows, PAGE_SIZE)),
            const((tri_n, tri_n)),
            pl.BlockSpec((1, dec_l, SB_WIDTH), lambda b, j, pt: (b, 0, COL_QA)),
            pl.BlockSpec((1, dec_l, SB_WIDTH), lambda b, j, pt: (b, 0, COL_KA)),
            pl.BlockSpec((1, dec_l, SB_WIDTH), lambda b, j, pt: (b, 0, COL_VA)),
        ] + [page_spec(p) for p in range(DEC_PAGES)] * 2,
        out_specs=pl.BlockSpec((1, dec_l, SB_WIDTH), lambda b, j, pt: (b, 0, 0)),
        scratch_shapes=[
            pltpu.VMEM((rows, SB_DIM), F32),
            pltpu.VMEM((rows, SB_WIDTH), BF16),
            pltpu.VMEM((PAGE_SIZE, SB_WIDTH), BF16),
            pltpu.VMEM((PAGE_SIZE, SB_WIDTH), BF16),
            pltpu.VMEM((dec_l, SB_WIDTH), F32),
            pltpu.VMEM((rows, SB_DIM), F32),
            pltpu.VMEM((rows, LANES), F32),
        ],
    )
    return pl.pallas_call(
        _sb_decode_kernel,
        grid_spec=grid_spec,
        out_shape=jax.ShapeDtypeStruct((dec_b, dec_l, SB_WIDTH), F32),
        compiler_params=pltpu.CompilerParams(
            dimension_semantics=("parallel", "arbitrary"), vmem_limit_bytes=VMEM_LIMIT_BYTES),
        name="sb_decode",
    )(page_table, page_bias, new_bias, _tri(tri_n), branch, branch, branch,
      *([cache_k] * DEC_PAGES), *([cache_v] * DEC_PAGES))


def _log_gamma():
    return jnp.log(1.0 - jnp.exp2(-5.0 - jnp.arange(RET_HEADS, dtype=F32)))


def _rope_tables(pos):
    half = RET_DIM // 2
    freqs = ROPE_BASE ** (-jnp.arange(half, dtype=F32) / half)
    ang = pos.astype(F32)[:, None] * freqs[None, :]
    cos, sin = jnp.cos(ang), jnp.sin(ang)
    return jnp.concatenate([cos, cos], axis=1), jnp.concatenate([-sin, sin], axis=1)


def _decay_tables(length):
    lg = _log_gamma()
    idx = jnp.arange(length, dtype=F32)
    rel = idx[:, None] - idx[None, :]
    intra = jnp.where(rel[None] >= 0, jnp.exp(jnp.maximum(rel, 0.0)[None] * lg[:, None, None]), 0.0)
    q_decay = jnp.exp((idx + 1.0)[None, :] * lg[:, None])
    k_decay = jnp.exp((length - 1.0 - idx)[None, :] * lg[:, None])
    rep = lambda t: jnp.broadcast_to(t[:, :, None], (RET_HEADS, length, LANES))
    return intra, rep(q_decay), rep(k_decay), jnp.exp(length * lg)


def _rope(x, cos2, sin2):
    return x * cos2 + pltpu.roll(x, RET_DIM // 2, axis=1) * sin2


def _groupnorm_gate(o, gn, g):
    mu = jnp.mean(o, axis=-1, keepdims=True)
    d = o - mu
    var = jnp.mean(d * d, axis=-1, keepdims=True)
    return jax.nn.silu(g) * (d * lax.rsqrt(var + EPS) * gn)


def _ret_prompt_kernel(sdec_ref, cos_ref, sin_ref, intra_ref, qdec_ref, kdec_ref, gn_ref,
                       q_ref, k_ref, v_ref, g_ref, r_ref, sfin_ref, state_ref):
    c = pl.program_id(1)

    @pl.when(c == 0)
    def _():
        state_ref[...] = jnp.zeros_like(state_ref)

    cos2, sin2 = cos_ref[...], sin_ref[...]
    for h in range(RET_HEADS):
        cols = slice(h * RET_DIM, (h + 1) * RET_DIM)
        q = _rope(q_ref[0, :, cols].astype(F32), cos2, sin2)
        k = _rope(k_ref[0, :, cols].astype(F32), cos2, sin2) * RET_SCALE
        v = v_ref[0, :, cols]
        qb = q.astype(BF16)
        state = state_ref[h]
        scores = _dot_nt(qb, k.astype(BF16)) * intra_ref[h]
        o = _dot(scores.astype(BF16), v) + _dot(qb, state.astype(BF16)) * qdec_ref[h]
        state_ref[h] = sdec_ref[h] * state + _dot_tn((k * kdec_ref[h]).astype(BF16), v)
        r_ref[0, :, cols] = _groupnorm_gate(
            o, gn_ref[h], g_ref[0, :, cols].astype(F32)).astype(r_ref.dtype)

    @pl.when(c == pl.num_programs(1) - 1)
    def _():
        sfin_ref[0] = state_ref[...]


def _ret_prompt(branch, ret_gn, batch, seq):
    ch = RET_CHUNK
    cos2, sin2 = _rope_tables(jnp.arange(seq, dtype=jnp.int32))
    intra, q_decay, k_decay, s_decay = _decay_tables(ch)
    col = lambda cb: (lambda b, c: (b, c, cb))
    blk = (1, ch, RET_WIDTH)
    return pl.pallas_call(
        _ret_prompt_kernel,
        grid=(batch, seq // ch),
        in_specs=[
            pl.BlockSpec(memory_space=pltpu.SMEM),
            pl.BlockSpec((ch, RET_DIM), lambda b, c: (c, 0)),
            pl.BlockSpec((ch, RET_DIM), lambda b, c: (c, 0)),
            _const_spec((RET_HEADS, ch, ch)),
            _const_spec((RET_HEADS, ch, LANES)),
            _const_spec((RET_HEADS, ch, LANES)),
            _const_spec((RET_HEADS, 1, RET_DIM)),
            pl.BlockSpec(blk, col(COL_QR)),
            pl.BlockSpec(blk, col(COL_KR)),
            pl.BlockSpec(blk, col(COL_VR)),
            pl.BlockSpec(blk, col(COL_GR)),
        ],
        out_specs=[
            pl.BlockSpec(blk, lambda b, c: (b, c, 0)),
            pl.BlockSpec((1, RET_HEADS, RET_DIM, RET_DIM), lambda b, c: (b, 0, 0, 0)),
        ],
        out_shape=[
            jax.ShapeDtypeStruct((batch, seq, RET_WIDTH), BF16),
            jax.ShapeDtypeStruct((batch, RET_HEADS, RET_DIM, RET_DIM), F32),
        ],
        scratch_shapes=[pltpu.VMEM((RET_HEADS, RET_DIM, RET_DIM), F32)],
        compiler_params=pltpu.CompilerParams(
            dimension_semantics=("parallel", "arbitrary"), vmem_limit_bytes=VMEM_LIMIT_BYTES),
        name="ret_prompt",
    )(s_decay, cos2, sin2, intra, q_decay, k_decay, ret_gn.reshape(RET_HEADS, 1, RET_DIM),
      branch, branch, branch, branch)


def _ret_decode_kernel(sdec_ref, cos_ref, sin_ref, intra_ref, qdec_ref, kdec_ref, gn_ref,
                       q_ref, k_ref, v_ref, g_ref, s_ref, r_ref, snew_ref):
    n_seq, dec_l, _ = q_ref.shape
    cos2, sin2 = cos_ref[...], sin_ref[...]

    def one_seq(s, _):
        for h in range(RET_HEADS):
            cols = slice(h * RET_DIM, (h + 1) * RET_DIM)
            q = _rope(q_ref[s, :, cols], cos2, sin2)
            k = _rope(k_ref[s, :, cols], cos2, sin2) * RET_SCALE
            v = v_ref[s, :, cols].astype(BF16)
            qb = q.astype(BF16)
            state = s_ref[s, h]
            scores = _dot_nt(qb, k.astype(BF16)) * intra_ref[h]
            o = _dot(scores.astype(BF16), v) + _dot(qb, state.astype(BF16)) * qdec_ref[h]
            snew_ref[s, h] = sdec_ref[h] * state + _dot_tn((k * kdec_ref[h]).astype(BF16), v)
            r_ref[s, :, cols] = _groupnorm_gate(o, gn_ref[h], g_ref[s, :, cols])
        return 0

    lax.fori_loop(0, n_seq, one_seq, 0)


def _ret_decode(branch, state, ret_gn, past):
    dec_b, dec_l, _ = branch.shape
    g = DEC_SEQS
    cos2, sin2 = _rope_tables(past + jnp.arange(dec_l, dtype=jnp.int32))
    intra, q_decay, k_decay, s_decay = _decay_tables(dec_l)
    col = lambda cb: (lambda i: (i, 0, cb))
    blk = (g, dec_l, RET_WIDTH)
    sblk = (g, RET_HEADS, RET_DIM, RET_DIM)
    return pl.pallas_call(
        _ret_decode_kernel,
        grid=(dec_b // g,),
        in_specs=[
            pl.BlockSpec(memory_space=pltpu.SMEM),
            _const_spec((dec_l, RET_DIM)),
            _const_spec((dec_l, RET_DIM)),
            _const_spec((RET_HEADS, dec_l, dec_l)),
            _const_spec((RET_HEADS, dec_l, LANES)),
            _const_spec((RET_HEADS, dec_l, LANES)),
            _const_spec((RET_HEADS, 1, RET_DIM)),
            pl.BlockSpec(blk, col(COL_QR)),
            pl.BlockSpec(blk, col(COL_KR)),
            pl.BlockSpec(blk, col(COL_VR)),
            pl.BlockSpec(blk, col(COL_GR)),
            pl.BlockSpec(sblk, lambda i: (i, 0, 0, 0)),
        ],
        out_specs=[
            pl.BlockSpec(blk, lambda i: (i, 0, 0)),
            pl.BlockSpec(sblk, lambda i: (i, 0, 0, 0)),
        ],
        out_shape=[
            jax.ShapeDtypeStruct((dec_b, dec_l, RET_WIDTH), F32),
            jax.ShapeDtypeStruct(state.shape, state.dtype),
        ],
        compiler_params=pltpu.CompilerParams(
            dimension_semantics=("parallel",), vmem_limit_bytes=VMEM_LIMIT_BYTES),
        name="ret_decode",
    )(s_decay, cos2, sin2, intra, q_decay, k_decay, ret_gn.reshape(RET_HEADS, 1, RET_DIM),
      branch, branch, branch, branch, state)


def _post_kernel(oa_ref, r_ref, gates_ref, x_ref, p_ref, wsb_ref, wret_ref, wo_ref, gffn_ref,
                 wup_ref, wdown_ref, gple_ref, wpg_ref, wple_ref, gfin_ref, y_ref, *, final_norm):
    y_a = _dot(oa_ref[...].astype(BF16), wsb_ref[...])
    y_r = _dot(r_ref[...].astype(BF16), wret_ref[...])
    gate_a = gates_ref[:, :D_MODEL].astype(F32)
    gate_r = gates_ref[:, D_MODEL:].astype(F32)
    merged = jax.nn.sigmoid(gate_a) * y_a + jax.nn.sigmoid(gate_r) * y_r
    h = x_ref[...] + _dot(merged.astype(BF16), wo_ref[...])

    u = _rms(h, gffn_ref[...]).astype(BF16)
    for c in range(D_FF // D_MODEL):
        cols = slice(c * D_MODEL, (c + 1) * D_MODEL)
        act = jnp.square(jnp.maximum(_dot(u, wup_ref[:, cols]), 0.0))
        h = h + _dot(act.astype(BF16), wdown_ref[cols, :])

    u = _rms(h, gple_ref[...]).astype(BF16)
    gate = jax.nn.sigmoid(_dot(u, wpg_ref[...]))
    h = h + gate * _dot(p_ref[...].astype(BF16), wple_ref[...])
    y_ref[...] = _rms(h, gfin_ref[...]) if final_norm else h


def _post(o_a, r, gates, x, p, w, g_final, final_norm):
    t = x.shape[0]
    tm = POST_ROWS
    row = lambda i: (i, 0)
    weights = [w["w_sb_out"], w["w_ret_out"], w["w_o"], w["g_ffn"], w["w_up"], w["w_down"],
               w["g_ple"], w["w_ple_gate"], w["w_ple"], g_final]
    acts = [o_a, r, gates, x, p]
    return pl.pallas_call(
        functools.partial(_post_kernel, final_norm=final_norm),
        grid=(t // tm,),
        in_specs=[pl.BlockSpec((tm, a.shape[1]), row) for a in acts]
        + [_const_spec(a.shape) for a in weights],
        out_specs=pl.BlockSpec((tm, D_MODEL), row),
        out_shape=jax.ShapeDtypeStruct((t, D_MODEL), F32),
        compiler_params=pltpu.CompilerParams(
            dimension_semantics=("parallel",), vmem_limit_bytes=VMEM_LIMIT_BYTES),
        name="post",
    )(*acts, *weights)


def kernel(x_prompt, x_sample, cache_k, cache_v, state_ret, page_table, p_prompt, p_sample,
           g_mix, w_in, sb_bias, w_sb_out, w_ret_out, w_o, ret_gn, g_ffn, w_up, w_down, g_ple,
           w_ple_gate, w_ple, g_final):
    depth = g_mix.shape[0]
    batch, seq, _ = x_prompt.shape
    dec_b, dec_l, _ = x_sample.shape
    past =page_table.shape[1] * PAGE_SIZE
    gain = lambda g: g.reshape(1, D_MODEL)

    hp = x_prompt.reshape(batch * seq, D_MODEL)
    hs = x_sample.reshape(dec_b * dec_l, D_MODEL)
    outs = {name: [] for name in ("kp", "vp", "sp", "ks", "vs", "ss")}
    for i in range(depth):
        last = i == depth - 1
        w_in_i = w_in[i].astype(BF16)
        w = {
            "w_sb_out": w_sb_out[i].astype(BF16), "w_ret_out": w_ret_out[i].astype(BF16),
            "w_o": w_o[i].astype(BF16), "g_ffn": gain(g_ffn[i]), "w_up": w_up[i].astype(BF16),
            "w_down": w_down[i].astype(BF16), "g_ple": gain(g_ple[i]),
            "w_ple_gate": w_ple_gate[i].astype(BF16), "w_ple": w_ple[i].astype(BF16),
        }

        branch, gates, k_a, v_a = _in_projection(hp, g_mix[i], w_in_i, BF16)
        branch3 = branch.reshape(batch, seq, BRANCH_WIDTH)
        o_a = _sb_prompt(branch3, sb_bias[i], batch, seq)
        r, s_fin = _ret_prompt(branch3, ret_gn[i], batch, seq)
        hp = _post(o_a.reshape(batch * seq, SB_WIDTH), r.reshape(batch * seq, RET_WIDTH), gates, hp,
                   p_prompt[i].reshape(batch * seq, -1), w, gain(g_final), last)
        outs["kp"].append(k_a.reshape(batch, seq, SB_HEADS, SB_DIM))
        outs["vp"].append(v_a.reshape(batch, seq, SB_HEADS, SB_DIM))
        outs["sp"].append(s_fin)

        branch, gates, k_a, v_a = _in_projection(hs, g_mix[i], w_in_i, F32)
        branch3 = branch.reshape(dec_b, dec_l, BRANCH_WIDTH)
        o_a = _sb_decode(branch3, cache_k, cache_v, i, page_table, sb_bias[i])
        r, s_new = _ret_decode(branch3, state_ret[i], ret_gn[i], past)
        hs = _post(o_a.reshape(dec_b * dec_l, SB_WIDTH), r.reshape(dec_b * dec_l, RET_WIDTH), gates,
                   hs, p_sample[i].reshape(dec_b * dec_l, -1), w, gain(g_final), last)
        outs["ks"].append(k_a.reshape(dec_b, dec_l, SB_HEADS, SB_DIM))
        outs["vs"].append(v_a.reshape(dec_b, dec_l, SB_HEADS, SB_DIM))
        outs["ss"].append(s_new)

    return (hp.reshape(batch, seq, D_MODEL), hs.reshape(dec_b, dec_l, D_MODEL),
            jnp.stack(outs["kp"]), jnp.stack(outs["vp"]), jnp.stack(outs["sp"]),
            jnp.stack(outs["ks"]), jnp.stack(outs["vs"]), jnp.stack(outs["ss"]))
```

```python
import functools

import jax
import jax.numpy as jnp
import numpy as np
from jax import lax
from jax.experimental import pallas as pl
from jax.experimental.pallas import tpu as pltpu

F32 = jnp.float32
BF16 = jnp.bfloat16

D_MODEL = 1024
SB_HEADS = 8
SB_DIM = 64
SB_WIDTH = SB_HEADS * SB_DIM
RET_HEADS = 4
RET_DIM = 128
RET_WIDTH = RET_HEADS * RET_DIM
D_FF = 4 * D_MODEL
PAGE_SIZE = 128
ROPE_BASE = 10000.0
EPS = 1e-6
SB_SCALE = SB_DIM ** -0.5
RET_SCALE = RET_DIM ** -0.5

BRANCH_COLS = 512
COL_QA, COL_KA, COL_VA, COL_QR, COL_KR, COL_VR, COL_GR = range(7)
N_BRANCH_BLOCKS = 7
BRANCH_WIDTH = N_BRANCH_BLOCKS * BRANCH_COLS
GATE_WIDTH = 2 * D_MODEL

LANES = 128
VMEM_LIMIT_BYTES = 56 * 1024 * 1024

PROJ_ROWS = 512
POST_ROWS = 256
SB_BLOCK = 256
RET_CHUNK = 128
DEC_PAGES = 16
DEC_SEQS = 8


def _const_spec(shape):
    return pl.BlockSpec(shape, lambda *_: (0,) * len(shape), pipeline_mode=pl.Buffered(1))


def _rms(x, g):
    return x * lax.rsqrt(jnp.mean(x * x, axis=-1, keepdims=True) + EPS) * g


def _dot(a, b):
    return jnp.dot(a, b, preferred_element_type=F32)


def _dot_nt(a, b):
    return lax.dot_general(a, b, (((1,), (1,)), ((), ())), preferred_element_type=F32)


def _dot_tn(a, b):
    return lax.dot_general(a, b, (((0,), (0,)), ((), ())), preferred_element_type=F32)


def _proj_kernel(x_ref, g_ref, w_ref, branch_ref, gates_ref, k_ref, v_ref, *, head_major_kv):
    u = _rms(x_ref[...], g_ref[...]).astype(BF16)
    for c in range(N_BRANCH_BLOCKS):
        cols = slice(c * BRANCH_COLS, (c + 1) * BRANCH_COLS)
        r = _dot(u, w_ref[:, cols])
        if c == COL_QA:
            r = r * SB_SCALE
        if c in (COL_KA, COL_VA):
            kv_ref = k_ref if c == COL_KA else v_ref
            if head_major_kv:
                kv_ref[0] = r.T.reshape(SB_HEADS, SB_DIM, r.shape[0])
            else:
                kv_ref[...] = r
        branch_ref[:, cols] = r.astype(branch_ref.dtype)
    for c in range(GATE_WIDTH // BRANCH_COLS):
        cols = slice(c * BRANCH_COLS, (c + 1) * BRANCH_COLS)
        wcols = slice(BRANCH_WIDTH + c * BRANCH_COLS, BRANCH_WIDTH + (c + 1) * BRANCH_COLS)
        gates_ref[:, cols] = _dot(u, w_ref[:, wcols]).astype(gates_ref.dtype)


def _in_projection(x, g_mix, w_in_bf16, act_dtype, head_major_seq=None):
    t = x.shape[0]
    tm = PROJ_ROWS
    row = lambda i: (i, 0)
    if head_major_seq is None:
        kv_spec = pl.BlockSpec((tm, SB_WIDTH), row)
        kv_shape = jax.ShapeDtypeStruct((t, SB_WIDTH), F32)
    else:
        tiles = head_major_seq // tm
        kv_spec = pl.BlockSpec((1, SB_HEADS, SB_DIM, tm), lambda i: (i // tiles, 0, 0, i % tiles))
        kv_shape = jax.ShapeDtypeStruct((t // head_major_seq, SB_HEADS, SB_DIM, head_major_seq), F32)
    return pl.pallas_call(
        functools.partial(_proj_kernel, head_major_kv=head_major_seq is not None),
        grid=(t // tm,),
        in_specs=[
            pl.BlockSpec((tm, D_MODEL), row),
            _const_spec((1, D_MODEL)),
            _const_spec(w_in_bf16.shape),
        ],
        out_specs=[
            pl.BlockSpec((tm, BRANCH_WIDTH), row),
            pl.BlockSpec((tm, GATE_WIDTH), row),
            kv_spec,
            kv_spec,
        ],
        out_shape=[
            jax.ShapeDtypeStruct((t, BRANCH_WIDTH), act_dtype),
            jax.ShapeDtypeStruct((t, GATE_WIDTH), act_dtype),
            kv_shape,
            kv_shape,
        ],
        compiler_params=pltpu.CompilerParams(
            dimension_semantics=("parallel",), vmem_limit_bytes=VMEM_LIMIT_BYTES),
        name="in_projection",
    )(x, g_mix.reshape(1, D_MODEL), w_in_bf16)


def _stick_block(z, tri, carry, mask):
    m, n = z.shape
    stay = jnp.maximum(z, 0.0) + jnp.log(1.0 + jnp.exp(-jnp.abs(z)))
    if mask is not None:
        stay = jnp.where(mask, stay, 0.0)
    right = _dot(stay.astype(BF16), tri)
    carry_n = jnp.concatenate([carry] * (n // LANES), axis=1)
    a = jnp.exp(z - stay - right - carry_n)
    if mask is not None:
        a = jnp.where(mask, a, 0.0)
    return a, jnp.broadcast_to(jnp.sum(stay, axis=1, keepdims=True), (m, LANES))


def _sb_prompt_kernel(bias_ref, tri_ref, q_ref, k_ref, v_ref, o_ref, acc_ref, carry_ref):
    blk = SB_BLOCK
    i = pl.program_id(1)
    row = lax.broadcasted_iota(jnp.int32, (blk, blk), 0)
    col = lax.broadcasted_iota(jnp.int32, (blk, blk), 1)
    causal = col < row
    low_half = lax.broadcasted_iota(jnp.int32, (blk, LANES), 1) < SB_DIM
    acc_ref[...] = jnp.zeros_like(acc_ref)
    carry_ref[...] = jnp.zeros_like(carry_ref)

    def key_block(kb, mask):
        start = pl.multiple_of(kb * blk, blk)
        tri = tri_ref[...]
        for pair in range(SB_HEADS // 2):
            cols = slice(pair * LANES, (pair + 1) * LANES)
            q2 = q_ref[0, :, cols]
            k2 = k_ref[0, pl.ds(start, blk), cols]
            v2 = v_ref[0, pl.ds(start, blk), cols]
            update = None
            for sub in range(2):
                head = 2 * pair + sub
                own = low_half if sub == 0 else ~low_half
                q_head = jnp.where(own, q2, jnp.zeros_like(q2))
                v_head = jnp.where(own, v2, jnp.zeros_like(v2))
                z = _dot_nt(q_head, k2) + bias_ref[head]
                a, stay_sum = _stick_block(z, tri, carry_ref[head], mask)
                carry_ref[head] += stay_sum
                d = _dot(a.astype(BF16), v_head)
                update = d if update is None else update + d
            acc_ref[pair] += update

    key_block(i, causal)

    def body(t, c):
        key_block(i - 1 - t, None)
        return c

    lax.fori_loop(0, i, body, 0)
    for pair in range(SB_HEADS // 2):
        o_ref[0, :, pair * LANES:(pair + 1) * LANES] = acc_ref[pair].astype(o_ref.dtype)


def _tri(n):
    idx = jnp.arange(n)
    return (idx[:, None] > idx[None, :]).astype(BF16)


def _sb_prompt(branch, sb_bias, batch, seq):
    blk = SB_BLOCK
    return pl.pallas_call(
        _sb_prompt_kernel,
        grid=(batch, seq // blk),
        in_specs=[
            pl.BlockSpec(memory_space=pltpu.SMEM),
            _const_spec((blk, blk)),
            pl.BlockSpec((1, blk, SB_WIDTH), lambda b, i: (b, i, COL_QA)),
            pl.BlockSpec((1, seq, SB_WIDTH), lambda b, i: (b, 0, COL_KA)),
            pl.BlockSpec((1, seq, SB_WIDTH), lambda b, i: (b, 0, COL_VA)),
        ],
        out_specs=pl.BlockSpec((1, blk, SB_WIDTH), lambda b, i: (b, i, 0)),
        out_shape=jax.ShapeDtypeStruct((batch, seq, SB_WIDTH), BF16),
        scratch_shapes=[
            pltpu.VMEM((SB_HEADS // 2, blk, LANES), F32),
            pltpu.VMEM((SB_HEADS, blk, LANES), F32),
        ],
        compiler_params=pltpu.CompilerParams(
            dimension_semantics=("parallel", "arbitrary"), vmem_limit_bytes=VMEM_LIMIT_BYTES),
        name="sb_prompt",
    )(sb_bias, _tri(blk), branch, branch, branch)


def _sb_decode_kernel(pt_ref, bias_ref, new_bias_ref, tri_ref, q_ref, kn_ref, vn_ref, *rest):
    del pt_ref
    k_pages = rest[:DEC_PAGES]
    v_pages = rest[DEC_PAGES:2 * DEC_PAGES]
    o_ref, qbd_ref, kpad_ref, vpad_ref, acc_ref, carry_ref = rest[2 * DEC_PAGES:]
    j = pl.program_id(1)
    dec_l = q_ref.shape[1]
    rows = SB_HEADS * dec_l

    @pl.when(j == 0)
    def _():
        q = q_ref[0]
        r_head = lax.broadcasted_iota(jnp.int32, (rows, SB_WIDTH), 0) // dec_l
        c_head = lax.broadcasted_iota(jnp.int32, (rows, SB_WIDTH), 1) // SB_DIM
        qbd_ref[...] = jnp.where(r_head == c_head, jnp.concatenate([q] * SB_HEADS, axis=0), 0.0)
        kpad_ref[...] = jnp.zeros_like(kpad_ref)
        vpad_ref[...] = jnp.zeros_like(vpad_ref)
        kpad_ref[0:dec_l, :] = kn_ref[0]
        vpad_ref[0:dec_l, :] = vn_ref[0]
        z = _dot_nt(qbd_ref[...], kpad_ref[...]) + new_bias_ref[...]
        a, stay_sum = _stick_block(z, tri_ref[:PAGE_SIZE, :PAGE_SIZE], jnp.zeros((rows, LANES), F32), None)
        carry_ref[...] = stay_sum
        acc_ref[...] = _dot(a, vpad_ref[...])

    carry = carry_ref[...]
    acc = acc_ref[...]
    bias = jnp.concatenate([bias_ref[...]] * 2, axis=1)
    for m in reversed(range(DEC_PAGES // 2)):
        kt2 = jnp.concatenate([k_pages[2 * m + s][0, 0].reshape(SB_WIDTH, PAGE_SIZE) for s in (0, 1)], axis=1)
        vt2 = jnp.concatenate([v_pages[2 * m + s][0, 0].reshape(SB_WIDTH, PAGE_SIZE) for s in (0, 1)], axis=1)
        a, stay_sum = _stick_block(_dot(qbd_ref[...], kt2) + bias, tri_ref[...], carry, None)
        carry = carry + stay_sum
        acc = acc + _dot_nt(a, vt2)
    carry_ref[...] = carry
    acc_ref[...] = acc

    @pl.when(j == pl.num_programs(1) - 1)
    def _():
        r_head = lax.broadcasted_iota(jnp.int32, (rows, SB_WIDTH), 0) // dec_l
        c_head = lax.broadcasted_iota(jnp.int32, (rows, SB_WIDTH), 1) // SB_DIM
        own = jnp.where(r_head == c_head, acc_ref[...], 0.0)
        out = own[0:dec_l]
        for h in range(1, SB_HEADS):
            out = out + own[h * dec_l:(h + 1) * dec_l]
        o_ref[0] = out


def _sb_decode(branch, cache_k, cache_v, layer, page_table, sb_bias):
    dec_b, dec_l, _ = branch.shape
    n_pages = page_table.shape[1]
    n_steps = n_pages // DEC_PAGES
    rows = SB_HEADS * dec_l
    tri_n = 2 * PAGE_SIZE

    cache_kt = cache_k.transpose(0, 1, 3, 4, 2)
    cache_vt = cache_v.transpose(0, 1, 3, 4, 2)

    row_head = jnp.arange(rows) // dec_l
    row_query = jnp.arange(rows) % dec_l
    row_bias = jnp.broadcast_to(sb_bias[row_head][:, None], (rows, PAGE_SIZE))
    new_bias = jnp.where(jnp.arange(PAGE_SIZE)[None, :] < row_query[:, None], row_bias, -1e30)

    def page_spec(p):
        def index(b, j, pt):
            return (layer, pt[b, (n_steps - 1 - j) * DEC_PAGES + p], 0, 0, 0)
        return pl.BlockSpec((1, 1, SB_HEADS, SB_DIM, PAGE_SIZE), index)

    const = lambda shape: pl.BlockSpec(shape, lambda b, j, pt: (0,) * len(shape),
                                       pipeline_mode=pl.Buffered(1))
    grid_spec = pltpu.PrefetchScalarGridSpec(
        num_scalar_prefetch=1,
        grid=(dec_b, n_steps),
        in_specs=[
            const((rows, PAGE_SIZE)),
            const((rows, PAGE_SIZE)),
            const((tri_n, tri_n)),
            pl.BlockSpec((1, dec_l, SB_WIDTH), lambda b, j, pt: (b, 0, COL_QA)),
            pl.BlockSpec((1, dec_l, SB_WIDTH), lambda b, j, pt: (b, 0, COL_KA)),
            pl.BlockSpec((1, dec_l, SB_WIDTH), lambda b, j, pt: (b, 0, COL_VA)),
        ] + [page_spec(p) for p in range(DEC_PAGES)] * 2,
        out_specs=pl.BlockSpec((1, dec_l, SB_WIDTH), lambda b, j, pt: (b, 0, 0)),
        scratch_shapes=[
            pltpu.VMEM((rows, SB_WIDTH), F32),
            pltpu.VMEM((PAGE_SIZE, SB_WIDTH), F32),
            pltpu.VMEM((PAGE_SIZE, SB_WIDTH), F32),
            pltpu.VMEM((rows, SB_WIDTH), F32),
            pltpu.VMEM((rows, LANES), F32),
        ],
    )
    return pl.pallas_call(
        _sb_decode_kernel,
        grid_spec=grid_spec,
        out_shape=jax.ShapeDtypeStruct((dec_b, dec_l, SB_WIDTH), F32),
        compiler_params=pltpu.CompilerParams(
            dimension_semantics=("parallel", "arbitrary"), vmem_limit_bytes=VMEM_LIMIT_BYTES),
        name="sb_decode",
    )(page_table, row_bias, new_bias, _tri(tri_n), branch, branch, branch,
      *([cache_kt] * DEC_PAGES), *([cache_vt] * DEC_PAGES))


def _log_gamma():
    return jnp.log(1.0 - jnp.exp2(-5.0 - jnp.arange(RET_HEADS, dtype=F32)))


def _rope_tables(pos):
    half = RET_DIM // 2
    freqs = ROPE_BASE ** (-jnp.arange(half, dtype=F32) / half)
    ang = pos.astype(F32)[:, None] * freqs[None, :]
    cos, sin = jnp.cos(ang), jnp.sin(ang)
    return jnp.concatenate([cos, cos], axis=1), jnp.concatenate([-sin, sin], axis=1)


def _decay_tables(length):
    lg = _log_gamma()
    idx = jnp.arange(length, dtype=F32)
    rel = idx[:, None] - idx[None, :]
    intra = jnp.where(rel[None] >= 0, jnp.exp(jnp.maximum(rel, 0.0)[None] * lg[:, None, None]), 0.0)
    q_decay = jnp.exp((idx + 1.0)[None, :] * lg[:, None])
    k_decay = jnp.exp((length - 1.0 - idx)[None, :] * lg[:, None])
    rep = lambda t: jnp.broadcast_to(t[:, :, None], (RET_HEADS, length, LANES))
    return intra, rep(q_decay), rep(k_decay), jnp.exp(length * lg)


def _rope(x, cos2, sin2):
    return x * cos2 + pltpu.roll(x, RET_DIM // 2, axis=1) * sin2


def _groupnorm_gate(o, gn, g):
    mu = jnp.mean(o, axis=-1, keepdims=True)
    d = o - mu
    var = jnp.mean(d * d, axis=-1, keepdims=True)
    return jax.nn.silu(g) * (d * lax.rsqrt(var + EPS) * gn)


def _ret_prompt_kernel(sdec_ref, cos_ref, sin_ref, intra_ref, qdec_ref, kdec_ref, gn_ref,
                       q_ref, k_ref, v_ref, g_ref, r_ref, sfin_ref, state_ref):
    c = pl.program_id(1)

    @pl.when(c == 0)
    def _():
        state_ref[...] = jnp.zeros_like(state_ref)

    cos2, sin2 = cos_ref[...], sin_ref[...]
    for h in range(RET_HEADS):
        cols = slice(h * RET_DIM, (h + 1) * RET_DIM)
        q = _rope(q_ref[0, :, cols].astype(F32), cos2, sin2)
        k = _rope(k_ref[0, :, cols].astype(F32), cos2, sin2) * RET_SCALE
        v = v_ref[0, :, cols]
        qb = q.astype(BF16)
        state = state_ref[h]
        scores = _dot_nt(qb, k.astype(BF16)) * intra_ref[h]
        o = _dot(scores.astype(BF16), v) + _dot(qb, state.astype(BF16)) * qdec_ref[h]
        state_ref[h] = sdec_ref[h] * state + _dot_tn((k * kdec_ref[h]).astype(BF16), v)
        r_ref[0, :, cols] = _groupnorm_gate(
            o, gn_ref[h], g_ref[0, :, cols].astype(F32)).astype(r_ref.dtype)

    @pl.when(c == pl.num_programs(1) - 1)
    def _():
        sfin_ref[0] = state_ref[...]


def _ret_prompt(branch, ret_gn, batch, seq):
    ch = RET_CHUNK
    cos2, sin2 = _rope_tables(jnp.arange(seq, dtype=jnp.int32))
    intra, q_decay, k_decay, s_decay = _decay_tables(ch)
    col = lambda cb: (lambda b, c: (b, c, cb))
    blk = (1, ch, RET_WIDTH)
    return pl.pallas_call(
        _ret_prompt_kernel,
        grid=(batch, seq // ch),
        in_specs=[
            pl.BlockSpec(memory_space=pltpu.SMEM),
            pl.BlockSpec((ch, RET_DIM), lambda b, c: (c, 0)),
            pl.BlockSpec((ch, RET_DIM), lambda b, c: (c, 0)),
            _const_spec((RET_HEADS, ch, ch)),
            _const_spec((RET_HEADS, ch, LANES)),
            _const_spec((RET_HEADS, ch, LANES)),
            _const_spec((RET_HEADS, 1, RET_DIM)),
            pl.BlockSpec(blk, col(COL_QR)),
            pl.BlockSpec(blk, col(COL_KR)),
            pl.BlockSpec(blk, col(COL_VR)),
            pl.BlockSpec(blk, col(COL_GR)),
        ],
        out_specs=[
            pl.BlockSpec(blk, lambda b, c: (b, c, 0)),
            pl.BlockSpec((1, RET_HEADS, RET_DIM, RET_DIM), lambda b, c: (b, 0, 0, 0)),
        ],
        out_shape=[
            jax.ShapeDtypeStruct((batch, seq, RET_WIDTH), BF16),
            jax.ShapeDtypeStruct((batch, RET_HEADS, RET_DIM, RET_DIM), F32),
        ],
        scratch_shapes=[pltpu.VMEM((RET_HEADS, RET_DIM, RET_DIM), F32)],
        compiler_params=pltpu.CompilerParams(
            dimension_semantics=("parallel", "arbitrary"), vmem_limit_bytes=VMEM_LIMIT_BYTES),
        name="ret_prompt",
    )(s_decay, cos2, sin2, intra, q_decay, k_decay, ret_gn.reshape(RET_HEADS, 1, RET_DIM),
      branch, branch, branch, branch)


def _ret_decode_kernel(sdec_ref, cos_ref, sin_ref, intra_ref, qdec_ref, kdec_ref, gn_ref,
                       q_ref, k_ref, v_ref, g_ref, s_ref, r_ref, snew_ref):
    n_seq, dec_l, _ = q_ref.shape
    cos2, sin2 = cos_ref[...], sin_ref[...]

    def one_seq(s, _):
        for h in range(RET_HEADS):
            cols = slice(h * RET_DIM, (h + 1) * RET_DIM)
            q = _rope(q_ref[s, :, cols], cos2, sin2)
            k = _rope(k_ref[s, :, cols], cos2, sin2) * RET_SCALE
            v = v_ref[s, :, cols].astype(BF16)
            qb = q.astype(BF16)
            state = s_ref[s, h]
            scores = _dot_nt(qb, k.astype(BF16)) * intra_ref[h]
            o = _dot(scores.astype(BF16), v) + _dot(qb, state.astype(BF16)) * qdec_ref[h]
            snew_ref[s, h] = sdec_ref[h] * state + _dot_tn((k * kdec_ref[h]).astype(BF16), v)
            r_ref[s, :, cols] = _groupnorm_gate(o, gn_ref[h], g_ref[s, :, cols])
        return 0

    lax.fori_loop(0, n_seq, one_seq, 0)


def _ret_decode(branch, state, ret_gn, past):
    dec_b, dec_l, _ = branch.shape
    g = DEC_SEQS
    cos2, sin2 = _rope_tables(past + jnp.arange(dec_l, dtype=jnp.int32))
    intra, q_decay, k_decay, s_decay = _decay_tables(dec_l)
    col = lambda cb: (lambda i: (i, 0, cb))
    blk = (g, dec_l, RET_WIDTH)
    sblk = (g, RET_HEADS, RET_DIM, RET_DIM)
    return pl.pallas_call(
        _ret_decode_kernel,
        grid=(dec_b // g,),
        in_specs=[
            pl.BlockSpec(memory_space=pltpu.SMEM),
            _const_spec((dec_l, RET_DIM)),
            _const_spec((dec_l, RET_DIM)),
            _const_spec((RET_HEADS, dec_l, dec_l)),
            _const_spec((RET_HEADS, dec_l, LANES)),
            _const_spec((RET_HEADS, dec_l, LANES)),
            _const_spec((RET_HEADS, 1, RET_DIM)),
            pl.BlockSpec(blk, col(COL_QR)),
            pl.BlockSpec(blk, col(COL_KR)),
            pl.BlockSpec(blk, col(COL_VR)),
            pl.BlockSpec(blk, col(COL_GR)),
            pl.BlockSpec(sblk, lambda i: (i, 0, 0, 0)),
        ],
        out_specs=[
            pl.BlockSpec(blk, lambda i: (i, 0, 0)),
            pl.BlockSpec(sblk, lambda i: (i, 0, 0, 0)),
        ],
        out_shape=[
            jax.ShapeDtypeStruct((dec_b, dec_l, RET_WIDTH), F32),
            jax.ShapeDtypeStruct(state.shape, state.dtype),
        ],
        compiler_params=pltpu.CompilerParams(
            dimension_semantics=("parallel",), vmem_limit_bytes=VMEM_LIMIT_BYTES),
        name="ret_decode",
    )(s_decay, cos2, sin2, intra, q_decay, k_decay, ret_gn.reshape(RET_HEADS, 1, RET_DIM),
      branch, branch, branch, branch, state)


def _post_kernel(oa_ref, r_ref, gates_ref, x_ref, p_ref, wsb_ref, wret_ref, wo_ref, gffn_ref,
                 wup_ref, wdown_ref, gple_ref, wpg_ref, wple_ref, gfin_ref, y_ref, *, final_norm):
    y_a = _dot(oa_ref[...].astype(BF16), wsb_ref[...])
    y_r = _dot(r_ref[...].astype(BF16), wret_ref[...])
    gate_a = gates_ref[:, :D_MODEL].astype(F32)
    gate_r = gates_ref[:, D_MODEL:].astype(F32)
    merged = jax.nn.sigmoid(gate_a) * y_a + jax.nn.sigmoid(gate_r) * y_r
    h = x_ref[...] + _dot(merged.astype(BF16), wo_ref[...])

    u = _rms(h, gffn_ref[...]).astype(BF16)
    for c in range(D_FF // D_MODEL):
        cols = slice(c * D_MODEL, (c + 1) * D_MODEL)
        act = jnp.square(jnp.maximum(_dot(u, wup_ref[:, cols]), 0.0))
        h = h + _dot(act.astype(BF16), wdown_ref[cols, :])

    u = _rms(h, gple_ref[...]).astype(BF16)
    gate = jax.nn.sigmoid(_dot(u, wpg_ref[...]))
    h = h + gate * _dot(p_ref[...].astype(BF16), wple_ref[...])
    y_ref[...] = _rms(h, gfin_ref[...]) if final_norm else h


def _post(o_a, r, gates, x, p, w, g_final, final_norm):
    t = x.shape[0]
    tm = POST_ROWS
    row = lambda i: (i, 0)
    weights = [w["w_sb_out"], w["w_ret_out"], w["w_o"], w["g_ffn"], w["w_up"], w["w_down"],
               w["g_ple"], w["w_ple_gate"], w["w_ple"], g_final]
    acts = [o_a, r, gates, x, p]
    return pl.pallas_call(
        functools.partial(_post_kernel, final_norm=final_norm),
        grid=(t // tm,),
        in_specs=[pl.BlockSpec((tm, a.shape[1]), row) for a in acts]
        + [_const_spec(a.shape) for a in weights],
        out_specs=pl.BlockSpec((tm, D_MODEL), row),
        out_shape=jax.ShapeDtypeStruct((t, D_MODEL), F32),
        compiler_params=pltpu.CompilerParams(
            dimension_semantics=("parallel",), vmem_limit_bytes=VMEM_LIMIT_BYTES),
        name="post",
    )(*acts, *weights)


def kernel(x_prompt, x_sample, cache_k, cache_v, state_ret, page_table, p_prompt, p_sample,
           g_mix, w_in, sb_bias, w_sb_out, w_ret_out, w_o, ret_gn, g_ffn, w_up, w_down, g_ple,
           w_ple_gate, w_ple, g_final):
    depth = g_mix.shape[0]
    batch, seq, _ = x_prompt.shape
    dec_b, dec_l, _ = x_sample.shape
    past =page_table.shape[1] * PAGE_SIZE
    gain = lambda g: g.reshape(1, D_MODEL)

    hp = x_prompt.reshape(batch * seq, D_MODEL)
    hs = x_sample.reshape(dec_b * dec_l, D_MODEL)
    outs = {name: [] for name in ("kp", "vp", "sp", "ks", "vs", "ss")}
    for i in range(depth):
        last = i == depth - 1
        w_in_i = w_in[i].astype(BF16)
        w = {
            "w_sb_out": w_sb_out[i].astype(BF16), "w_ret_out": w_ret_out[i].astype(BF16),
            "w_o": w_o[i].astype(BF16), "g_ffn": gain(g_ffn[i]), "w_up": w_up[i].astype(BF16),
            "w_down": w_down[i].astype(BF16), "g_ple": gain(g_ple[i]),
            "w_ple_gate": w_ple_gate[i].astype(BF16), "w_ple": w_ple[i].astype(BF16),
        }

        branch, gates, k_t, v_t = _in_projection(hp, g_mix[i], w_in_i, BF16, head_major_seq=seq)
        branch3 = branch.reshape(batch, seq, BRANCH_WIDTH)
        o_a = _sb_prompt(branch3, sb_bias[i], batch, seq)
        r, s_fin = _ret_prompt(branch3, ret_gn[i], batch, seq)
        hp = _post(o_a.reshape(batch * seq, SB_WIDTH), r.reshape(batch * seq, RET_WIDTH), gates, hp,
                   p_prompt[i].reshape(batch * seq, -1), w, gain(g_final), last)
        outs["kp"].append(k_t.transpose(0, 3, 1, 2))
        outs["vp"].append(v_t.transpose(0, 3, 1, 2))
        outs["sp"].append(s_fin)

        branch, gates, k_a, v_a = _in_projection(hs, g_mix[i], w_in_i, F32)
        branch3 = branch.reshape(dec_b, dec_l, BRANCH_WIDTH)
        o_a = _sb_decode(branch3, cache_k, cache_v, i, page_table, sb_bias[i])
        r, s_new = _ret_decode(branch3, state_ret[i], ret_gn[i], past)
        hs = _post(o_a.reshape(dec_b * dec_l, SB_WIDTH), r.reshape(dec_b * dec_l, RET_WIDTH), gates,
                   hs, p_sample[i].reshape(dec_b * dec_l, -1), w, gain(g_final), last)
        outs["ks"].append(k_a.reshape(dec_b, dec_l, SB_HEADS, SB_DIM))
        outs["vs"].append(v_a.reshape(dec_b, dec_l, SB_HEADS, SB_DIM))
        outs["ss"].append(s_new)

    return (hp.reshape(batch, seq, D_MODEL), hs.reshape(dec_b, dec_l, D_MODEL),
            jnp.stack(outs["kp"]), jnp.stack(outs["vp"]), jnp.stack(outs["sp"]),
            jnp.stack(outs["ks"]), jnp.stack(outs["vs"]), jnp.stack(outs["ss"]))
```

```python
import functools

import jax
import jax.numpy as jnp
import numpy as np
from jax import lax
from jax.experimental import pallas as pl
from jax.experimental.pallas import tpu as pltpu

F32 = jnp.float32
BF16 = jnp.bfloat16

D_MODEL = 1024
SB_HEADS = 8
SB_DIM = 64
SB_WIDTH = SB_HEADS * SB_DIM
RET_HEADS = 4
RET_DIM = 128
RET_WIDTH = RET_HEADS * RET_DIM
D_FF = 4 * D_MODEL
PAGE_SIZE = 128
ROPE_BASE = 10000.0
EPS = 1e-6
SB_SCALE = SB_DIM ** -0.5
RET_SCALE = RET_DIM ** -0.5
LOG2E = 1.4426950408889634

BRANCH_COLS = 512
COL_QA, COL_KA, COL_VA, COL_QR, COL_KR, COL_VR, COL_GR = range(7)
N_BRANCH_BLOCKS = 7
BRANCH_WIDTH = N_BRANCH_BLOCKS * BRANCH_COLS
GATE_WIDTH = 2 * D_MODEL

LANES = 128
VMEM_LIMIT_BYTES = 56 * 1024 * 1024

PROJ_ROWS = 512
POST_ROWS = 256
SB_BLOCK = 256
RET_CHUNK = 256
DEC_PAGES = 16
DEC_SEQS = 8


def _const_spec(shape):
    return pl.BlockSpec(shape, lambda *_: (0,) * len(shape), pipeline_mode=pl.Buffered(1))


def _rms(x, g):
    return x * lax.rsqrt(jnp.mean(x * x, axis=-1, keepdims=True) + EPS) * g


def _dot(a, b):
    return jnp.dot(a, b, preferred_element_type=F32)


def _dot_nt(a, b):
    return lax.dot_general(a, b, (((1,), (1,)), ((), ())), preferred_element_type=F32)


def _dot_tn(a, b):
    return lax.dot_general(a, b, (((0,), (0,)), ((), ())), preferred_element_type=F32)


def _proj_kernel(x_ref, g_ref, w_ref, branch_ref, gates_ref, k_ref, v_ref, *, head_major_kv):
    u = _rms(x_ref[...], g_ref[...]).astype(BF16)
    for c in range(N_BRANCH_BLOCKS):
        cols = slice(c * BRANCH_COLS, (c + 1) * BRANCH_COLS)
        r = _dot(u, w_ref[:, cols])
        if c == COL_QA:
            r = r * SB_SCALE
        if c in (COL_KA, COL_VA):
            kv_ref = k_ref if c == COL_KA else v_ref
            if head_major_kv:
                kv_ref[0] = r.T.reshape(SB_HEADS, SB_DIM, r.shape[0])
            else:
                kv_ref[...] = r
        branch_ref[:, cols] = r.astype(branch_ref.dtype)
    for c in range(GATE_WIDTH // BRANCH_COLS):
        cols = slice(c * BRANCH_COLS, (c + 1) * BRANCH_COLS)
        wcols = slice(BRANCH_WIDTH + c * BRANCH_COLS, BRANCH_WIDTH + (c + 1) * BRANCH_COLS)
        gates_ref[:, cols] = _dot(u, w_ref[:, wcols]).astype(gates_ref.dtype)


def _in_projection(x, g_mix, w_in_bf16, act_dtype, head_major_seq=None):
    t = x.shape[0]
    tm = PROJ_ROWS
    row = lambda i: (i, 0)
    if head_major_seq is None:
        kv_spec = pl.BlockSpec((tm, SB_WIDTH), row)
        kv_shape = jax.ShapeDtypeStruct((t, SB_WIDTH), F32)
    else:
        tiles = head_major_seq // tm
        kv_spec = pl.BlockSpec((1, SB_HEADS, SB_DIM, tm), lambda i: (i // tiles, 0, 0, i % tiles))
        kv_shape = jax.ShapeDtypeStruct((t // head_major_seq, SB_HEADS, SB_DIM, head_major_seq), F32)
    return pl.pallas_call(
        functools.partial(_proj_kernel, head_major_kv=head_major_seq is not None),
        grid=(t // tm,),
        in_specs=[
            pl.BlockSpec((tm, D_MODEL), row),
            _const_spec((1, D_MODEL)),
            _const_spec(w_in_bf16.shape),
        ],
        out_specs=[
            pl.BlockSpec((tm, BRANCH_WIDTH), row),
            pl.BlockSpec((tm, GATE_WIDTH), row),
            kv_spec,
            kv_spec,
        ],
        out_shape=[
            jax.ShapeDtypeStruct((t, BRANCH_WIDTH), act_dtype),
            jax.ShapeDtypeStruct((t, GATE_WIDTH), act_dtype),
            kv_shape,
            kv_shape,
        ],
        compiler_params=pltpu.CompilerParams(
            dimension_semantics=("parallel",), vmem_limit_bytes=VMEM_LIMIT_BYTES),
        name="in_projection",
    )(x, g_mix.reshape(1, D_MODEL), w_in_bf16)


def _stick_weights(z, tri, carry, mask=None):
    m, n = z.shape
    t = tri.shape[0]
    stay = jnp.maximum(z, 0.0) + jnp.log(1.0 + jnp.exp2(jnp.abs(z) * (-LOG2E)))
    if mask is not None:
        stay = jnp.where(mask, stay, 0.0)
    chunks = [stay[:, c * t:(c + 1) * t] for c in range(n // t)]
    right = _dot(jnp.concatenate(chunks, axis=0).astype(BF16), tri)
    weights = [None] * len(chunks)
    for c in reversed(range(len(chunks))):
        carry_t = jnp.concatenate([carry] * (t // LANES), axis=1)
        a = jnp.exp(z[:, c * t:(c + 1) * t] - chunks[c] - right[c * m:(c + 1) * m] - carry_t)
        weights[c] = a if mask is None else jnp.where(mask, a, 0.0)
        carry = carry + jnp.broadcast_to(jnp.sum(chunks[c], axis=1, keepdims=True), (m, LANES))
    return jnp.concatenate(weights, axis=1), carry


def _sb_prompt_kernel(bias_ref, tri_ref, q_ref, k_ref, v_ref, o_ref, acc_ref, carry_ref):
    blk = SB_BLOCK
    i = pl.program_id(1)
    row = lax.broadcasted_iota(jnp.int32, (blk, blk), 0)
    col = lax.broadcasted_iota(jnp.int32, (blk, blk), 1)
    causal = col < row
    acc_ref[...] = jnp.zeros_like(acc_ref)
    carry_ref[...] = jnp.zeros_like(carry_ref)

    def key_blocks(first, n_blocks, mask=None):
        start = pl.multiple_of(first * blk, n_blocks * blk)
        tri = tri_ref[...]
        low_q = lax.broadcasted_iota(jnp.int32, (blk, LANES), 1) < SB_DIM
        low_v = lax.broadcasted_iota(jnp.int32, (n_blocks * blk, LANES), 1) < SB_DIM
        for pair in range(SB_HEADS // 2):
            cols = slice(pair * LANES, (pair + 1) * LANES)
            q2 = q_ref[0, :, cols]
            k2 = k_ref[0, pl.ds(start, n_blocks * blk), cols]
            v2 = v_ref[0, pl.ds(start, n_blocks * blk), cols]
            update = None
            for sub in range(2):
                head = 2 * pair + sub
                q_head = jnp.where(low_q if sub == 0 else ~low_q, q2, jnp.zeros_like(q2))
                v_head = jnp.where(low_v if sub == 0 else ~low_v, v2, jnp.zeros_like(v2))
                z = _dot_nt(q_head, k2) + bias_ref[head]
                a, carry_ref[head] = _stick_weights(z, tri, carry_ref[head], mask)
                d = _dot(a.astype(BF16), v_head)
                update = d if update is None else update + d
            acc_ref[pair] += update

    key_blocks(i, 1, causal)

    @pl.when(i % 2 == 1)
    def _():
        key_blocks(i - 1, 1)

    def body(t, c):
        key_blocks((i // 2 - 1 - t) * 2, 2)
        return c

    lax.fori_loop(0, i // 2, body, 0)
    for pair in range(SB_HEADS // 2):
        o_ref[0, :, pair * LANES:(pair + 1) * LANES] = acc_ref[pair].astype(o_ref.dtype)


def _tri(n):
    idx = jnp.arange(n)
    return (idx[:, None] > idx[None, :]).astype(BF16)


def _sb_prompt(branch, sb_bias, batch, seq):
    blk = SB_BLOCK
    return pl.pallas_call(
        _sb_prompt_kernel,
        grid=(batch, seq // blk),
        in_specs=[
            pl.BlockSpec(memory_space=pltpu.SMEM),
            _const_spec((blk, blk)),
            pl.BlockSpec((1, blk, SB_WIDTH), lambda b, i: (b, i, COL_QA)),
            pl.BlockSpec((1, seq, SB_WIDTH), lambda b, i: (b, 0, COL_KA)),
            pl.BlockSpec((1, seq, SB_WIDTH), lambda b, i: (b, 0, COL_VA)),
        ],
        out_specs=pl.BlockSpec((1, blk, SB_WIDTH), lambda b, i: (b, i, 0)),
        out_shape=jax.ShapeDtypeStruct((batch, seq, SB_WIDTH), BF16),
        scratch_shapes=[
            pltpu.VMEM((SB_HEADS // 2, blk, LANES), F32),
            pltpu.VMEM((SB_HEADS, blk, LANES), F32),
        ],
        compiler_params=pltpu.CompilerParams(
            dimension_semantics=("parallel", "arbitrary"), vmem_limit_bytes=VMEM_LIMIT_BYTES),
        name="sb_prompt",
    )(sb_bias, _tri(blk), branch, branch, branch)


def _sb_decode_kernel(pt_ref, bias_ref, new_bias_ref, tri_ref, q_ref, kn_ref, vn_ref, *rest):
    del pt_ref
    k_pages = rest[:DEC_PAGES]
    v_pages = rest[DEC_PAGES:2 * DEC_PAGES]
    o_ref, qbd_ref, kpad_ref, vpad_ref, acc_ref, carry_ref = rest[2 * DEC_PAGES:]
    j = pl.program_id(1)
    dec_l = q_ref.shape[1]
    rows = SB_HEADS * dec_l

    @pl.when(j == 0)
    def _():
        q = q_ref[0]
        r_head = lax.broadcasted_iota(jnp.int32, (rows, SB_WIDTH), 0) // dec_l
        c_head = lax.broadcasted_iota(jnp.int32, (rows, SB_WIDTH), 1) // SB_DIM
        qbd_ref[...] = jnp.where(r_head == c_head, jnp.concatenate([q] * SB_HEADS, axis=0), 0.0)
        kpad_ref[...] = jnp.zeros_like(kpad_ref)
        vpad_ref[...] = jnp.zeros_like(vpad_ref)
        kpad_ref[0:dec_l, :] = kn_ref[0]
        vpad_ref[0:dec_l, :] = vn_ref[0]
        z = _dot_nt(qbd_ref[...], kpad_ref[...]) + new_bias_ref[...]
        a, carry_ref[...] = _stick_weights(
            z, tri_ref[:PAGE_SIZE, :PAGE_SIZE], jnp.zeros((rows, LANES), F32))
        acc_ref[...] = _dot(a, vpad_ref[...])

    kt = jnp.concatenate([r[0, 0].reshape(SB_WIDTH, PAGE_SIZE) for r in k_pages], axis=1)
    vt = jnp.concatenate([r[0, 0].reshape(SB_WIDTH, PAGE_SIZE) for r in v_pages], axis=1)
    z = _dot(qbd_ref[...], kt) + jnp.concatenate([bias_ref[...]] * DEC_PAGES, axis=1)
    a, carry_ref[...] = _stick_weights(z, tri_ref[...], carry_ref[...])
    acc_ref[...] += _dot_nt(a, vt)

    @pl.when(j == pl.num_programs(1) - 1)
    def _():
        r_head = lax.broadcasted_iota(jnp.int32, (rows, SB_WIDTH), 0) // dec_l
        c_head = lax.broadcasted_iota(jnp.int32, (rows, SB_WIDTH), 1) // SB_DIM
        own = jnp.where(r_head == c_head, acc_ref[...], 0.0)
        out = own[0:dec_l]
        for h in range(1, SB_HEADS):
            out = out + own[h * dec_l:(h + 1) * dec_l]
        o_ref[0] = out


def _sb_decode(branch, cache_k, cache_v, layer, page_table, sb_bias):
    dec_b, dec_l, _ = branch.shape
    n_pages = page_table.shape[1]
    n_steps = n_pages // DEC_PAGES
    rows = SB_HEADS * dec_l
    tri_n = 2 * PAGE_SIZE

    cache_kt = cache_k.transpose(0, 1, 3, 4, 2)
    cache_vt = cache_v.transpose(0, 1, 3, 4, 2)

    row_head = jnp.arange(rows) // dec_l
    row_query = jnp.arange(rows) % dec_l
    row_bias = jnp.broadcast_to(sb_bias[row_head][:, None], (rows, PAGE_SIZE))
    new_bias = jnp.where(jnp.arange(PAGE_SIZE)[None, :] < row_query[:, None], row_bias, -1e30)

    def page_spec(p):
        def index(b, j, pt):
            return (layer, pt[b, (n_steps - 1 - j) * DEC_PAGES + p], 0, 0, 0)
        return pl.BlockSpec((1, 1, SB_HEADS, SB_DIM, PAGE_SIZE), index)

    const = lambda shape: pl.BlockSpec(shape, lambda b, j, pt: (0,) * len(shape),
                                       pipeline_mode=pl.Buffered(1))
    grid_spec = pltpu.PrefetchScalarGridSpec(
        num_scalar_prefetch=1,
        grid=(dec_b, n_steps),
        in_specs=[
            const((rows, PAGE_SIZE)),
            const((rows, PAGE_SIZE)),
            const((tri_n, tri_n)),
            pl.BlockSpec((1, dec_l, SB_WIDTH), lambda b, j, pt: (b, 0, COL_QA)),
            pl.BlockSpec((1, dec_l, SB_WIDTH), lambda b, j, pt: (b, 0, COL_KA)),
            pl.BlockSpec((1, dec_l, SB_WIDTH), lambda b, j, pt: (b, 0, COL_VA)),
        ] + [page_spec(p) for p in range(DEC_PAGES)] * 2,
        out_specs=pl.BlockSpec((1, dec_l, SB_WIDTH), lambda b, j, pt: (b, 0, 0)),
        scratch_shapes=[
            pltpu.VMEM((rows, SB_WIDTH), F32),
            pltpu.VMEM((PAGE_SIZE, SB_WIDTH), F32),
            pltpu.VMEM((PAGE_SIZE, SB_WIDTH), F32),
            pltpu.VMEM((rows, SB_WIDTH), F32),
            pltpu.VMEM((rows, LANES), F32),
        ],
    )
    return pl.pallas_call(
        _sb_decode_kernel,
        grid_spec=grid_spec,
        out_shape=jax.ShapeDtypeStruct((dec_b, dec_l, SB_WIDTH), F32),
        compiler_params=pltpu.CompilerParams(
            dimension_semantics=("parallel", "arbitrary"), vmem_limit_bytes=VMEM_LIMIT_BYTES),
        name="sb_decode",
    )(page_table, row_bias, new_bias, _tri(tri_n), branch, branch, branch,
      *([cache_kt] * DEC_PAGES), *([cache_vt] * DEC_PAGES))


def _log_gamma():
    return jnp.log(1.0 - jnp.exp2(-5.0 - jnp.arange(RET_HEADS, dtype=F32)))


def _rope_tables(pos):
    half = RET_DIM // 2
    freqs = ROPE_BASE ** (-jnp.arange(half, dtype=F32) / half)
    ang = pos.astype(F32)[:, None] * freqs[None, :]
    cos, sin = jnp.cos(ang), jnp.sin(ang)
    return jnp.concatenate([cos, cos], axis=1), jnp.concatenate([-sin, sin], axis=1)


def _decay_tables(length):
    lg = _log_gamma()
    idx = jnp.arange(length, dtype=F32)
    rel = idx[:, None] - idx[None, :]
    intra = jnp.where(rel[None] >= 0, jnp.exp(jnp.maximum(rel, 0.0)[None] * lg[:, None, None]), 0.0)
    q_decay = jnp.exp((idx + 1.0)[None, :] * lg[:, None])
    k_decay = jnp.exp((length - 1.0 - idx)[None, :] * lg[:, None])
    rep = lambda t: jnp.broadcast_to(t[:, :, None], (RET_HEADS, length, LANES))
    return intra, rep(q_decay), rep(k_decay), jnp.exp(length * lg)


def _rope(x, cos2, sin2):
    return x * cos2 + pltpu.roll(x, RET_DIM // 2, axis=1) * sin2


def _groupnorm_gate(o, gn, g):
    mu = jnp.mean(o, axis=-1, keepdims=True)
    d = o - mu
    var = jnp.mean(d * d, axis=-1, keepdims=True)
    return jax.nn.silu(g) * (d * lax.rsqrt(var + EPS) * gn)


def _ret_prompt_kernel(sdec_ref, cos_ref, sin_ref, intra_ref, qdec_ref, kdec_ref, gn_ref,
                       q_ref, k_ref, v_ref, g_ref, r_ref, sfin_ref, state_ref):
    c = pl.program_id(1)

    @pl.when(c == 0)
    def _():
        state_ref[...] = jnp.zeros_like(state_ref)

    cos2, sin2 = cos_ref[...], sin_ref[...]
    for h in range(RET_HEADS):
        cols = slice(h * RET_DIM, (h + 1) * RET_DIM)
        q = _rope(q_ref[0, :, cols].astype(F32), cos2, sin2)
        k = _rope(k_ref[0, :, cols].astype(F32), cos2, sin2) * RET_SCALE
        v = v_ref[0, :, cols]
        qb = q.astype(BF16)
        state = state_ref[h]
        scores = _dot_nt(qb, k.astype(BF16)) * intra_ref[h]
        o = _dot(scores.astype(BF16), v) + _dot(qb, state.astype(BF16)) * qdec_ref[h]
        state_ref[h] = sdec_ref[h] * state + _dot_tn((k * kdec_ref[h]).astype(BF16), v)
        r_ref[0, :, cols] = _groupnorm_gate(
            o, gn_ref[h], g_ref[0, :, cols].astype(F32)).astype(r_ref.dtype)

    @pl.when(c == pl.num_programs(1) - 1)
    def _():
        sfin_ref[0] = state_ref[...]


def _ret_prompt(branch, ret_gn, batch, seq):
    ch = RET_CHUNK
    cos2, sin2 = _rope_tables(jnp.arange(seq, dtype=jnp.int32))
    intra, q_decay, k_decay, s_decay = _decay_tables(ch)
    col = lambda cb: (lambda b, c: (b, c, cb))
    blk = (1, ch, RET_WIDTH)
    return pl.pallas_call(
        _ret_prompt_kernel,
        grid=(batch, seq // ch),
        in_specs=[
            pl.BlockSpec(memory_space=pltpu.SMEM),
            pl.BlockSpec((ch, RET_DIM), lambda b, c: (c, 0)),
            pl.BlockSpec((ch, RET_DIM), lambda b, c: (c, 0)),
            _const_spec((RET_HEADS, ch, ch)),
            _const_spec((RET_HEADS, ch, LANES)),
            _const_spec((RET_HEADS, ch, LANES)),
            _const_spec((RET_HEADS, 1, RET_DIM)),
            pl.BlockSpec(blk, col(COL_QR)),
            pl.BlockSpec(blk, col(COL_KR)),
            pl.BlockSpec(blk, col(COL_VR)),
            pl.BlockSpec(blk, col(COL_GR)),
        ],
        out_specs=[
            pl.BlockSpec(blk, lambda b, c: (b, c, 0)),
            pl.BlockSpec((1, RET_HEADS, RET_DIM, RET_DIM), lambda b, c: (b, 0, 0, 0)),
        ],
        out_shape=[
            jax.ShapeDtypeStruct((batch, seq, RET_WIDTH), BF16),
            jax.ShapeDtypeStruct((batch, RET_HEADS, RET_DIM, RET_DIM), F32),
        ],
        scratch_shapes=[pltpu.VMEM((RET_HEADS, RET_DIM, RET_DIM), F32)],
        compiler_params=pltpu.CompilerParams(
            dimension_semantics=("parallel", "arbitrary"), vmem_limit_bytes=VMEM_LIMIT_BYTES),
        name="ret_prompt",
    )(s_decay, cos2, sin2, intra, q_decay, k_decay, ret_gn.reshape(RET_HEADS, 1, RET_DIM),
      branch, branch, branch, branch)


def _ret_decode_kernel(sdec_ref, cos_ref, sin_ref, intra_ref, qdec_ref, kdec_ref, gn_ref,
                       q_ref, k_ref, v_ref, g_ref, s_ref, r_ref, snew_ref):
    n_seq, dec_l, _ = q_ref.shape
    cos2, sin2 = cos_ref[...], sin_ref[...]

    def one_seq(s, _):
        for h in range(RET_HEADS):
            cols = slice(h * RET_DIM, (h + 1) * RET_DIM)
            q = _rope(q_ref[s, :, cols], cos2, sin2)
            k = _rope(k_ref[s, :, cols], cos2, sin2) * RET_SCALE
            v = v_ref[s, :, cols].astype(BF16)
            qb = q.astype(BF16)
            state = s_ref[s, h]
            scores = _dot_nt(qb, k.astype(BF16)) * intra_ref[h]
            o = _dot(scores.astype(BF16), v) + _dot(qb, state.astype(BF16)) * qdec_ref[h]
            snew_ref[s, h] = sdec_ref[h] * state + _dot_tn((k * kdec_ref[h]).astype(BF16), v)
            r_ref[s, :, cols] = _groupnorm_gate(o, gn_ref[h], g_ref[s, :, cols])
        return 0

    lax.fori_loop(0, n_seq, one_seq, 0)


def _ret_decode(branch, state, ret_gn, past):
    dec_b, dec_l, _ = branch.shape
    g = DEC_SEQS
    cos2, sin2 = _rope_tables(past + jnp.arange(dec_l, dtype=jnp.int32))
    intra, q_decay, k_decay, s_decay = _decay_tables(dec_l)
    col = lambda cb: (lambda i: (i, 0, cb))
    blk = (g, dec_l, RET_WIDTH)
    sblk = (g, RET_HEADS, RET_DIM, RET_DIM)
    return pl.pallas_call(
        _ret_decode_kernel,
        grid=(dec_b // g,),
        in_specs=[
            pl.BlockSpec(memory_space=pltpu.SMEM),
            _const_spec((dec_l, RET_DIM)),
            _const_spec((dec_l, RET_DIM)),
            _const_spec((RET_HEADS, dec_l, dec_l)),
            _const_spec((RET_HEADS, dec_l, LANES)),
            _const_spec((RET_HEADS, dec_l, LANES)),
            _const_spec((RET_HEADS, 1, RET_DIM)),
            pl.BlockSpec(blk, col(COL_QR)),
            pl.BlockSpec(blk, col(COL_KR)),
            pl.BlockSpec(blk, col(COL_VR)),
            pl.BlockSpec(blk, col(COL_GR)),
            pl.BlockSpec(sblk, lambda i: (i, 0, 0, 0)),
        ],
        out_specs=[
            pl.BlockSpec(blk, lambda i: (i, 0, 0)),
            pl.BlockSpec(sblk, lambda i: (i, 0, 0, 0)),
        ],
        out_shape=[
            jax.ShapeDtypeStruct((dec_b, dec_l, RET_WIDTH), F32),
            jax.ShapeDtypeStruct(state.shape, state.dtype),
        ],
        compiler_params=pltpu.CompilerParams(
            dimension_semantics=("parallel",), vmem_limit_bytes=VMEM_LIMIT_BYTES),
        name="ret_decode",
    )(s_decay, cos2, sin2, intra, q_decay, k_decay, ret_gn.reshape(RET_HEADS, 1, RET_DIM),
      branch, branch, branch, branch, state)


def _post_kernel(oa_ref, r_ref, gates_ref, x_ref, p_ref, wsb_ref, wret_ref, wo_ref, gffn_ref,
                 wup_ref, wdown_ref, gple_ref, wpg_ref, wple_ref, gfin_ref, y_ref, *, final_norm):
    y_a = _dot(oa_ref[...].astype(BF16), wsb_ref[...])
    y_r = _dot(r_ref[...].astype(BF16), wret_ref[...])
    gate_a = gates_ref[:, :D_MODEL].astype(F32)
    gate_r = gates_ref[:, D_MODEL:].astype(F32)
    merged = jax.nn.sigmoid(gate_a) * y_a + jax.nn.sigmoid(gate_r) * y_r
    h = x_ref[...] + _dot(merged.astype(BF16), wo_ref[...])

    u = _rms(h, gffn_ref[...]).astype(BF16)
    for c in range(D_FF // D_MODEL):
        cols = slice(c * D_MODEL, (c + 1) * D_MODEL)
        act = jnp.square(jnp.maximum(_dot(u, wup_ref[:, cols]), 0.0))
        h = h + _dot(act.astype(BF16), wdown_ref[cols, :])

    u = _rms(h, gple_ref[...]).astype(BF16)
    gate = jax.nn.sigmoid(_dot(u, wpg_ref[...]))
    h = h + gate * _dot(p_ref[...].astype(BF16), wple_ref[...])
    y_ref[...] = _rms(h, gfin_ref[...]) if final_norm else h


def _post(o_a, r, gates, x, p, w, g_final, final_norm):
    t = x.shape[0]
    tm = POST_ROWS
    row = lambda i: (i, 0)
    weights = [w["w_sb_out"], w["w_ret_out"], w["w_o"], w["g_ffn"], w["w_up"], w["w_down"],
               w["g_ple"], w["w_ple_gate"], w["w_ple"], g_final]
    acts = [o_a, r, gates, x, p]
    return pl.pallas_call(
        functools.partial(_post_kernel, final_norm=final_norm),
        grid=(t // tm,),
        in_specs=[pl.BlockSpec((tm, a.shape[1]), row) for a in acts]
        + [_const_spec(a.shape) for a in weights],
        out_specs=pl.BlockSpec((tm, D_MODEL), row),
        out_shape=jax.ShapeDtypeStruct((t, D_MODEL), F32),
        compiler_params=pltpu.CompilerParams(
            dimension_semantics=("parallel",), vmem_limit_bytes=VMEM_LIMIT_BYTES),
        name="post",
    )(*acts, *weights)


def kernel(x_prompt, x_sample, cache_k, cache_v, state_ret, page_table, p_prompt, p_sample,
           g_mix, w_in, sb_bias, w_sb_out, w_ret_out, w_o, ret_gn, g_ffn, w_up, w_down, g_ple,
           w_ple_gate, w_ple, g_final):
    depth = g_mix.shape[0]
    batch, seq, _ = x_prompt.shape
    dec_b, dec_l, _ = x_sample.shape
    past =page_table.shape[1] * PAGE_SIZE
    gain = lambda g: g.reshape(1, D_MODEL)

    hp = x_prompt.reshape(batch * seq, D_MODEL)
    hs = x_sample.reshape(dec_b * dec_l, D_MODEL)
    outs = {name: [] for name in ("kp", "vp", "sp", "ks", "vs", "ss")}
    for i in range(depth):
        last = i == depth - 1
        w_in_i = w_in[i].astype(BF16)
        w = {
            "w_sb_out": w_sb_out[i].astype(BF16), "w_ret_out": w_ret_out[i].astype(BF16),
            "w_o": w_o[i].astype(BF16), "g_ffn": gain(g_ffn[i]), "w_up": w_up[i].astype(BF16),
            "w_down": w_down[i].astype(BF16), "g_ple": gain(g_ple[i]),
            "w_ple_gate": w_ple_gate[i].astype(BF16), "w_ple": w_ple[i].astype(BF16),
        }

        branch, gates, k_t, v_t = _in_projection(hp, g_mix[i], w_in_i, BF16, head_major_seq=seq)
        branch3 = branch.reshape(batch, seq, BRANCH_WIDTH)
        o_a = _sb_prompt(branch3, sb_bias[i], batch, seq)
        r, s_fin = _ret_prompt(branch3, ret_gn[i], batch, seq)
        hp = _post(o_a.reshape(batch * seq, SB_WIDTH), r.reshape(batch * seq, RET_WIDTH), gates, hp,
                   p_prompt[i].reshape(batch * seq, -1), w, gain(g_final), last)
        outs["kp"].append(k_t.transpose(0, 3, 1, 2))
        outs["vp"].append(v_t.transpose(0, 3, 1, 2))
        outs["sp"].append(s_fin)

        branch, gates, k_a, v_a = _in_projection(hs, g_mix[i], w_in_i, F32)
        branch3 = branch.reshape(dec_b, dec_l, BRANCH_WIDTH)
        o_a = _sb_decode(branch3, cache_k, cache_v, i, page_table, sb_bias[i])
        r, s_new = _ret_decode(branch3, state_ret[i], ret_gn[i], past)
        hs = _post(o_a.reshape(dec_b * dec_l, SB_WIDTH), r.reshape(dec_b * dec_l, RET_WIDTH), gates,
                   hs, p_sample[i].reshape(dec_b * dec_l, -1), w, gain(g_final), last)
        outs["ks"].append(k_a.reshape(dec_b, dec_l, SB_HEADS, SB_DIM))
        outs["vs"].append(v_a.reshape(dec_b, dec_l, SB_HEADS, SB_DIM))
        outs["ss"].append(s_new)

    return (hp.reshape(batch, seq, D_MODEL), hs.reshape(dec_b, dec_l, D_MODEL),
            jnp.stack(outs["kp"]), jnp.stack(outs["vp"]), jnp.stack(outs["sp"]),
            jnp.stack(outs["ks"]), jnp.stack(outs["vs"]), jnp.stack(outs["ss"]))
```

```python
import functools

import jax
import jax.numpy as jnp
import numpy as np
from jax import lax
from jax.experimental import pallas as pl
from jax.experimental.pallas import tpu as pltpu

F32 = jnp.float32
BF16 = jnp.bfloat16

D_MODEL = 1024
SB_HEADS = 8
SB_DIM = 64
SB_WIDTH = SB_HEADS * SB_DIM
RET_HEADS = 4
RET_DIM = 128
RET_WIDTH = RET_HEADS * RET_DIM
D_FF = 4 * D_MODEL
PAGE_SIZE = 128
ROPE_BASE = 10000.0
EPS = 1e-6
SB_SCALE = SB_DIM ** -0.5
RET_SCALE = RET_DIM ** -0.5
LOG2E = 1.4426950408889634

BRANCH_COLS = 512
COL_QA, COL_KA, COL_VA, COL_QR, COL_KR, COL_VR, COL_GR = range(7)
N_BRANCH_BLOCKS = 7
BRANCH_WIDTH = N_BRANCH_BLOCKS * BRANCH_COLS
GATE_WIDTH = 2 * D_MODEL

LANES = 128
VMEM_LIMIT_BYTES = 56 * 1024 * 1024

PROJ_ROWS = 512
POST_ROWS = 256
SB_BLOCK = 256
RET_CHUNK = 256
DEC_PAGES = 32
DEC_SEQS = 8


def _const_spec(shape):
    return pl.BlockSpec(shape, lambda *_: (0,) * len(shape), pipeline_mode=pl.Buffered(1))


def _rms(x, g):
    return x * lax.rsqrt(jnp.mean(x * x, axis=-1, keepdims=True) + EPS) * g


def _dot(a, b):
    return jnp.dot(a, b, preferred_element_type=F32)


def _dot_nt(a, b):
    return lax.dot_general(a, b, (((1,), (1,)), ((), ())), preferred_element_type=F32)


def _dot_tn(a, b):
    return lax.dot_general(a, b, (((0,), (0,)), ((), ())), preferred_element_type=F32)


def _proj_kernel(x_ref, g_ref, w_ref, branch_ref, gates_ref, k_ref, v_ref, *, head_major_kv):
    u = _rms(x_ref[...], g_ref[...]).astype(BF16)
    for c in range(N_BRANCH_BLOCKS):
        cols = slice(c * BRANCH_COLS, (c + 1) * BRANCH_COLS)
        r = _dot(u, w_ref[:, cols])
        if c == COL_QA:
            r = r * SB_SCALE
        if c in (COL_KA, COL_VA):
            kv_ref = k_ref if c == COL_KA else v_ref
            if head_major_kv:
                kv_ref[0] = r.T.reshape(SB_HEADS, SB_DIM, r.shape[0])
            else:
                kv_ref[...] = r
        branch_ref[:, cols] = r.astype(branch_ref.dtype)
    for c in range(GATE_WIDTH // BRANCH_COLS):
        cols = slice(c * BRANCH_COLS, (c + 1) * BRANCH_COLS)
        wcols = slice(BRANCH_WIDTH + c * BRANCH_COLS, BRANCH_WIDTH + (c + 1) * BRANCH_COLS)
        gates_ref[:, cols] = _dot(u, w_ref[:, wcols]).astype(gates_ref.dtype)


def _in_projection(x, g_mix, w_in_bf16, act_dtype, head_major_seq=None):
    t = x.shape[0]
    tm = PROJ_ROWS
    row = lambda i: (i, 0)
    if head_major_seq is None:
        kv_spec = pl.BlockSpec((tm, SB_WIDTH), row)
        kv_shape = jax.ShapeDtypeStruct((t, SB_WIDTH), F32)
    else:
        tiles = head_major_seq // tm
        kv_spec = pl.BlockSpec((1, SB_HEADS, SB_DIM, tm), lambda i: (i // tiles, 0, 0, i % tiles))
        kv_shape = jax.ShapeDtypeStruct((t // head_major_seq, SB_HEADS, SB_DIM, head_major_seq), F32)
    return pl.pallas_call(
        functools.partial(_proj_kernel, head_major_kv=head_major_seq is not None),
        grid=(t // tm,),
        in_specs=[
            pl.BlockSpec((tm, D_MODEL), row),
            _const_spec((1, D_MODEL)),
            _const_spec(w_in_bf16.shape),
        ],
        out_specs=[
            pl.BlockSpec((tm, BRANCH_WIDTH), row),
            pl.BlockSpec((tm, GATE_WIDTH), row),
            kv_spec,
            kv_spec,
        ],
        out_shape=[
            jax.ShapeDtypeStruct((t, BRANCH_WIDTH), act_dtype),
            jax.ShapeDtypeStruct((t, GATE_WIDTH), act_dtype),
            kv_shape,
            kv_shape,
        ],
        compiler_params=pltpu.CompilerParams(
            dimension_semantics=("parallel",), vmem_limit_bytes=VMEM_LIMIT_BYTES),
        name="in_projection",
    )(x, g_mix.reshape(1, D_MODEL), w_in_bf16)


def _stick_weights(z, tri, carry, mask=None):
    m, n = z.shape
    t = tri.shape[0]
    stay = jnp.maximum(z, 0.0) + jnp.log(1.0 + jnp.exp2(jnp.abs(z) * (-LOG2E)))
    if mask is not None:
        stay = jnp.where(mask, stay, 0.0)
    chunks = [stay[:, c * t:(c + 1) * t] for c in range(n // t)]
    right = _dot(jnp.concatenate(chunks, axis=0).astype(BF16), tri)
    weights = [None] * len(chunks)
    for c in reversed(range(len(chunks))):
        carry_t = jnp.concatenate([carry] * (t // LANES), axis=1)
        a = jnp.exp(z[:, c * t:(c + 1) * t] - chunks[c] - right[c * m:(c + 1) * m] - carry_t)
        weights[c] = a if mask is None else jnp.where(mask, a, 0.0)
        carry = carry + jnp.broadcast_to(jnp.sum(chunks[c], axis=1, keepdims=True), (m, LANES))
    return jnp.concatenate(weights, axis=1), carry


def _sb_prompt_kernel(bias_ref, tri_ref, q_ref, k_ref, v_ref, o_ref, acc_ref, carry_ref):
    blk = SB_BLOCK
    i = pl.program_id(1)
    row = lax.broadcasted_iota(jnp.int32, (blk, blk), 0)
    col = lax.broadcasted_iota(jnp.int32, (blk, blk), 1)
    causal = col < row
    acc_ref[...] = jnp.zeros_like(acc_ref)
    carry_ref[...] = jnp.zeros_like(carry_ref)

    def key_blocks(first, n_blocks, mask=None):
        start = pl.multiple_of(first * blk, n_blocks * blk)
        tri = tri_ref[...]
        low_q = lax.broadcasted_iota(jnp.int32, (blk, LANES), 1) < SB_DIM
        low_v = lax.broadcasted_iota(jnp.int32, (n_blocks * blk, LANES), 1) < SB_DIM
        for pair in range(SB_HEADS // 2):
            cols = slice(pair * LANES, (pair + 1) * LANES)
            q2 = q_ref[0, :, cols]
            k2 = k_ref[0, pl.ds(start, n_blocks * blk), cols]
            v2 = v_ref[0, pl.ds(start, n_blocks * blk), cols]
            update = None
            for sub in range(2):
                head = 2 * pair + sub
                q_head = jnp.where(low_q if sub == 0 else ~low_q, q2, jnp.zeros_like(q2))
                v_head = jnp.where(low_v if sub == 0 else ~low_v, v2, jnp.zeros_like(v2))
                z = _dot_nt(q_head, k2) + bias_ref[head]
                a, carry_ref[head] = _stick_weights(z, tri, carry_ref[head], mask)
                d = _dot(a.astype(BF16), v_head)
                update = d if update is None else update + d
            acc_ref[pair] += update

    key_blocks(i, 1, causal)

    @pl.when(i % 2 == 1)
    def _():
        key_blocks(i - 1, 1)

    def body(t, c):
        key_blocks((i // 2 - 1 - t) * 2, 2)
        return c

    lax.fori_loop(0, i // 2, body, 0)
    for pair in range(SB_HEADS // 2):
        o_ref[0, :, pair * LANES:(pair + 1) * LANES] = acc_ref[pair].astype(o_ref.dtype)


def _tri(n):
    idx = jnp.arange(n)
    return (idx[:, None] > idx[None, :]).astype(BF16)


def _sb_prompt(branch, sb_bias, batch, seq):
    blk = SB_BLOCK
    return pl.pallas_call(
        _sb_prompt_kernel,
        grid=(batch, seq // blk),
        in_specs=[
            pl.BlockSpec(memory_space=pltpu.SMEM),
            _const_spec((blk, blk)),
            pl.BlockSpec((1, blk, SB_WIDTH), lambda b, i: (b, i, COL_QA)),
            pl.BlockSpec((1, seq, SB_WIDTH), lambda b, i: (b, 0, COL_KA)),
            pl.BlockSpec((1, seq, SB_WIDTH), lambda b, i: (b, 0, COL_VA)),
        ],
        out_specs=pl.BlockSpec((1, blk, SB_WIDTH), lambda b, i: (b, i, 0)),
        out_shape=jax.ShapeDtypeStruct((batch, seq, SB_WIDTH), BF16),
        scratch_shapes=[
            pltpu.VMEM((SB_HEADS // 2, blk, LANES), F32),
            pltpu.VMEM((SB_HEADS, blk, LANES), F32),
        ],
        compiler_params=pltpu.CompilerParams(
            dimension_semantics=("parallel", "arbitrary"), vmem_limit_bytes=VMEM_LIMIT_BYTES),
        name="sb_prompt",
    )(sb_bias, _tri(blk), branch, branch, branch)


def _sb_decode_kernel(pt_ref, bias_ref, new_bias_ref, tri_ref, q_ref, kn_ref, vn_ref, *rest):
    del pt_ref
    k_pages = rest[:DEC_PAGES]
    v_pages = rest[DEC_PAGES:2 * DEC_PAGES]
    o_ref, qbd_ref, kpad_ref, vpad_ref, acc_ref, carry_ref = rest[2 * DEC_PAGES:]
    j = pl.program_id(1)
    dec_l = q_ref.shape[1]
    rows = SB_HEADS * dec_l

    @pl.when(j == 0)
    def _():
        q = q_ref[0]
        r_head = lax.broadcasted_iota(jnp.int32, (rows, SB_WIDTH), 0) // dec_l
        c_head = lax.broadcasted_iota(jnp.int32, (rows, SB_WIDTH), 1) // SB_DIM
        qbd_ref[...] = jnp.where(r_head == c_head, jnp.concatenate([q] * SB_HEADS, axis=0), 0.0)
        kpad_ref[...] = jnp.zeros_like(kpad_ref)
        vpad_ref[...] = jnp.zeros_like(vpad_ref)
        kpad_ref[0:dec_l, :] = kn_ref[0]
        vpad_ref[0:dec_l, :] = vn_ref[0]
        z = _dot_nt(qbd_ref[...], kpad_ref[...]) + new_bias_ref[...]
        a, carry_ref[...] = _stick_weights(
            z, tri_ref[:PAGE_SIZE, :PAGE_SIZE], jnp.zeros((rows, LANES), F32))
        acc_ref[...] = _dot(a, vpad_ref[...])

    kt = jnp.concatenate([r[0, 0].reshape(SB_WIDTH, PAGE_SIZE) for r in k_pages], axis=1)
    vt = jnp.concatenate([r[0, 0].reshape(SB_WIDTH, PAGE_SIZE) for r in v_pages], axis=1)
    z = _dot(qbd_ref[...], kt) + jnp.concatenate([bias_ref[...]] * DEC_PAGES, axis=1)
    a, carry_ref[...] = _stick_weights(z, tri_ref[...], carry_ref[...])
    acc_ref[...] += _dot_nt(a, vt)

    @pl.when(j == pl.num_programs(1) - 1)
    def _():
        r_head = lax.broadcasted_iota(jnp.int32, (rows, SB_WIDTH), 0) // dec_l
        c_head = lax.broadcasted_iota(jnp.int32, (rows, SB_WIDTH), 1) // SB_DIM
        own = jnp.where(r_head == c_head, acc_ref[...], 0.0)
        out = own[0:dec_l]
        for h in range(1, SB_HEADS):
            out = out + own[h * dec_l:(h + 1) * dec_l]
        o_ref[0] = out


def _sb_decode(branch, cache_k, cache_v, layer, page_table, sb_bias):
    dec_b, dec_l, _ = branch.shape
    n_pages = page_table.shape[1]
    n_steps = n_pages // DEC_PAGES
    rows = SB_HEADS * dec_l
    tri_n = 2 * PAGE_SIZE

    cache_kt = cache_k.transpose(0, 1, 3, 4, 2)
    cache_vt = cache_v.transpose(0, 1, 3, 4, 2)

    row_head = jnp.arange(rows) // dec_l
    row_query = jnp.arange(rows) % dec_l
    row_bias = jnp.broadcast_to(sb_bias[row_head][:, None], (rows, PAGE_SIZE))
    new_bias = jnp.where(jnp.arange(PAGE_SIZE)[None, :] < row_query[:, None], row_bias, -1e30)

    def page_spec(p):
        def index(b, j, pt):
            return (layer, pt[b, (n_steps - 1 - j) * DEC_PAGES + p], 0, 0, 0)
        return pl.BlockSpec((1, 1, SB_HEADS, SB_DIM, PAGE_SIZE), index)

    const = lambda shape: pl.BlockSpec(shape, lambda b, j, pt: (0,) * len(shape),
                                       pipeline_mode=pl.Buffered(1))
    grid_spec = pltpu.PrefetchScalarGridSpec(
        num_scalar_prefetch=1,
        grid=(dec_b, n_steps),
        in_specs=[
            const((rows, PAGE_SIZE)),
            const((rows, PAGE_SIZE)),
            const((tri_n, tri_n)),
            pl.BlockSpec((1, dec_l, SB_WIDTH), lambda b, j, pt: (b, 0, COL_QA)),
            pl.BlockSpec((1, dec_l, SB_WIDTH), lambda b, j, pt: (b, 0, COL_KA)),
            pl.BlockSpec((1, dec_l, SB_WIDTH), lambda b, j, pt: (b, 0, COL_VA)),
        ] + [page_spec(p) for p in range(DEC_PAGES)] * 2,
        out_specs=pl.BlockSpec((1, dec_l, SB_WIDTH), lambda b, j, pt: (b, 0, 0)),
        scratch_shapes=[
            pltpu.VMEM((rows, SB_WIDTH), F32),
            pltpu.VMEM((PAGE_SIZE, SB_WIDTH), F32),
            pltpu.VMEM((PAGE_SIZE, SB_WIDTH), F32),
            pltpu.VMEM((rows, SB_WIDTH), F32),
            pltpu.VMEM((rows, LANES), F32),
        ],
    )
    return pl.pallas_call(
        _sb_decode_kernel,
        grid_spec=grid_spec,
        out_shape=jax.ShapeDtypeStruct((dec_b, dec_l, SB_WIDTH), F32),
        compiler_params=pltpu.CompilerParams(
            dimension_semantics=("parallel", "arbitrary"), vmem_limit_bytes=VMEM_LIMIT_BYTES),
        name="sb_decode",
    )(page_table, row_bias, new_bias, _tri(tri_n), branch, branch, branch,
      *([cache_kt] * DEC_PAGES), *([cache_vt] * DEC_PAGES))


def _log_gamma():
    return jnp.log(1.0 - jnp.exp2(-5.0 - jnp.arange(RET_HEADS, dtype=F32)))


def _rope_tables(pos):
    half = RET_DIM // 2
    freqs = ROPE_BASE ** (-jnp.arange(half, dtype=F32) / half)
    ang = pos.astype(F32)[:, None] * freqs[None, :]
    cos, sin = jnp.cos(ang), jnp.sin(ang)
    return jnp.concatenate([cos, cos], axis=1), jnp.concatenate([-sin, sin], axis=1)


def _decay_tables(length):
    lg = _log_gamma()
    idx = jnp.arange(length, dtype=F32)
    rel = idx[:, None] - idx[None, :]
    intra = jnp.where(rel[None] >= 0, jnp.exp(jnp.maximum(rel, 0.0)[None] * lg[:, None, None]), 0.0)
    q_decay = jnp.exp((idx + 1.0)[None, :] * lg[:, None])
    k_decay = jnp.exp((length - 1.0 - idx)[None, :] * lg[:, None])
    rep = lambda t: jnp.broadcast_to(t[:, :, None], (RET_HEADS, length, LANES))
    return intra, rep(q_decay), rep(k_decay), jnp.exp(length * lg)


def _rope(x, cos2, sin2):
    return x * cos2 + pltpu.roll(x, RET_DIM // 2, axis=1) * sin2


def _groupnorm_gate(o, gn, g):
    mu = jnp.mean(o, axis=-1, keepdims=True)
    d = o - mu
    var = jnp.mean(d * d, axis=-1, keepdims=True)
    return jax.nn.silu(g) * (d * lax.rsqrt(var + EPS) * gn)


def _ret_prompt_kernel(sdec_ref, cos_ref, sin_ref, intra_ref, qdec_ref, kdec_ref, gn_ref,
                       q_ref, k_ref, v_ref, g_ref, r_ref, sfin_ref, state_ref):
    c = pl.program_id(1)

    @pl.when(c == 0)
    def _():
        state_ref[...] = jnp.zeros_like(state_ref)

    cos2, sin2 = cos_ref[...], sin_ref[...]
    for h in range(RET_HEADS):
        cols = slice(h * RET_DIM, (h + 1) * RET_DIM)
        q = _rope(q_ref[0, :, cols].astype(F32), cos2, sin2)
        k = _rope(k_ref[0, :, cols].astype(F32), cos2, sin2) * RET_SCALE
        v = v_ref[0, :, cols]
        qb = q.astype(BF16)
        state = state_ref[h]
        scores = _dot_nt(qb, k.astype(BF16)) * intra_ref[h]
        o = _dot(scores.astype(BF16), v) + _dot(qb, state.astype(BF16)) * qdec_ref[h]
        state_ref[h] = sdec_ref[h] * state + _dot_tn((k * kdec_ref[h]).astype(BF16), v)
        r_ref[0, :, cols] = _groupnorm_gate(
            o, gn_ref[h], g_ref[0, :, cols].astype(F32)).astype(r_ref.dtype)

    @pl.when(c == pl.num_programs(1) - 1)
    def _():
        sfin_ref[0] = state_ref[...]


def _ret_prompt(branch, ret_gn, batch, seq):
    ch = RET_CHUNK
    cos2, sin2 = _rope_tables(jnp.arange(seq, dtype=jnp.int32))
    intra, q_decay, k_decay, s_decay = _decay_tables(ch)
    col = lambda cb: (lambda b, c: (b, c, cb))
    blk = (1, ch, RET_WIDTH)
    return pl.pallas_call(
        _ret_prompt_kernel,
        grid=(batch, seq // ch),
        in_specs=[
            pl.BlockSpec(memory_space=pltpu.SMEM),
            pl.BlockSpec((ch, RET_DIM), lambda b, c: (c, 0)),
            pl.BlockSpec((ch, RET_DIM), lambda b, c: (c, 0)),
            _const_spec((RET_HEADS, ch, ch)),
            _const_spec((RET_HEADS, ch, LANES)),
            _const_spec((RET_HEADS, ch, LANES)),
            _const_spec((RET_HEADS, 1, RET_DIM)),
            pl.BlockSpec(blk, col(COL_QR)),
            pl.BlockSpec(blk, col(COL_KR)),
            pl.BlockSpec(blk, col(COL_VR)),
            pl.BlockSpec(blk, col(COL_GR)),
        ],
        out_specs=[
            pl.BlockSpec(blk, lambda b, c: (b, c, 0)),
            pl.BlockSpec((1, RET_HEADS, RET_DIM, RET_DIM), lambda b, c: (b, 0, 0, 0)),
        ],
        out_shape=[
            jax.ShapeDtypeStruct((batch, seq, RET_WIDTH), BF16),
            jax.ShapeDtypeStruct((batch, RET_HEADS, RET_DIM, RET_DIM), F32),
        ],
        scratch_shapes=[pltpu.VMEM((RET_HEADS, RET_DIM, RET_DIM), F32)],
        compiler_params=pltpu.CompilerParams(
            dimension_semantics=("parallel", "arbitrary"), vmem_limit_bytes=VMEM_LIMIT_BYTES),
        name="ret_prompt",
    )(s_decay, cos2, sin2, intra, q_decay, k_decay, ret_gn.reshape(RET_HEADS, 1, RET_DIM),
      branch, branch, branch, branch)


def _ret_decode_kernel(sdec_ref, cos_ref, sin_ref, intra_ref, qdec_ref, kdec_ref, gn_ref,
                       q_ref, k_ref, v_ref, g_ref, s_ref, r_ref, snew_ref):
    n_seq, dec_l, _ = q_ref.shape
    cos2, sin2 = cos_ref[...], sin_ref[...]

    for s in range(n_seq):
        for h in range(RET_HEADS):
            cols = slice(h * RET_DIM, (h + 1) * RET_DIM)
            q = _rope(q_ref[s, :, cols], cos2, sin2)
            k = _rope(k_ref[s, :, cols], cos2, sin2) * RET_SCALE
            v = v_ref[s, :, cols].astype(BF16)
            qb = q.astype(BF16)
            state = s_ref[s, h]
            scores = _dot_nt(qb, k.astype(BF16)) * intra_ref[h]
            o = _dot(scores.astype(BF16), v) + _dot(qb, state.astype(BF16)) * qdec_ref[h]
            snew_ref[s, h] = sdec_ref[h] * state + _dot_tn((k * kdec_ref[h]).astype(BF16), v)
            r_ref[s, :, cols] = _groupnorm_gate(o, gn_ref[h], g_ref[s, :, cols])


def _ret_decode(branch, state, ret_gn, past):
    dec_b, dec_l, _ = branch.shape
    g = DEC_SEQS
    cos2, sin2 = _rope_tables(past + jnp.arange(dec_l, dtype=jnp.int32))
    intra, q_decay, k_decay, s_decay = _decay_tables(dec_l)
    col = lambda cb: (lambda i: (i, 0, cb))
    blk = (g, dec_l, RET_WIDTH)
    sblk = (g, RET_HEADS, RET_DIM, RET_DIM)
    return pl.pallas_call(
        _ret_decode_kernel,
        grid=(dec_b // g,),
        in_specs=[
            pl.BlockSpec(memory_space=pltpu.SMEM),
            _const_spec((dec_l, RET_DIM)),
            _const_spec((dec_l, RET_DIM)),
            _const_spec((RET_HEADS, dec_l, dec_l)),
            _const_spec((RET_HEADS, dec_l, LANES)),
            _const_spec((RET_HEADS, dec_l, LANES)),
            _const_spec((RET_HEADS, 1, RET_DIM)),
            pl.BlockSpec(blk, col(COL_QR)),
            pl.BlockSpec(blk, col(COL_KR)),
            pl.BlockSpec(blk, col(COL_VR)),
            pl.BlockSpec(blk, col(COL_GR)),
            pl.BlockSpec(sblk, lambda i: (i, 0, 0, 0)),
        ],
        out_specs=[
            pl.BlockSpec(blk, lambda i: (i, 0, 0)),
            pl.BlockSpec(sblk, lambda i: (i, 0, 0, 0)),
        ],
        out_shape=[
            jax.ShapeDtypeStruct((dec_b, dec_l, RET_WIDTH), F32),
            jax.ShapeDtypeStruct(state.shape, state.dtype),
        ],
        compiler_params=pltpu.CompilerParams(
            dimension_semantics=("parallel",), vmem_limit_bytes=VMEM_LIMIT_BYTES),
        name="ret_decode",
    )(s_decay, cos2, sin2, intra, q_decay, k_decay, ret_gn.reshape(RET_HEADS, 1, RET_DIM),
      branch, branch, branch, branch, state)


def _post_kernel(oa_ref, r_ref, gates_ref, x_ref, p_ref, wsb_ref, wret_ref, wo_ref, gffn_ref,
                 wup_ref, wdown_ref, gple_ref, wpg_ref, wple_ref, gfin_ref, y_ref, *, final_norm):
    y_a = _dot(oa_ref[...].astype(BF16), wsb_ref[...])
    y_r = _dot(r_ref[...].astype(BF16), wret_ref[...])
    gate_a = gates_ref[:, :D_MODEL].astype(F32)
    gate_r = gates_ref[:, D_MODEL:].astype(F32)
    merged = jax.nn.sigmoid(gate_a) * y_a + jax.nn.sigmoid(gate_r) * y_r
    h = x_ref[...] + _dot(merged.astype(BF16), wo_ref[...])

    u = _rms(h, gffn_ref[...]).astype(BF16)
    for c in range(D_FF // D_MODEL):
        cols = slice(c * D_MODEL, (c + 1) * D_MODEL)
        act = jnp.square(jnp.maximum(_dot(u, wup_ref[:, cols]), 0.0))
        h = h + _dot(act.astype(BF16), wdown_ref[cols, :])

    u = _rms(h, gple_ref[...]).astype(BF16)
    gate = jax.nn.sigmoid(_dot(u, wpg_ref[...]))
    h = h + gate * _dot(p_ref[...].astype(BF16), wple_ref[...])
    y_ref[...] = _rms(h, gfin_ref[...]) if final_norm else h


def _post(o_a, r, gates, x, p, w, g_final, final_norm):
    t = x.shape[0]
    tm = POST_ROWS
    row = lambda i: (i, 0)
    weights = [w["w_sb_out"], w["w_ret_out"], w["w_o"], w["g_ffn"], w["w_up"], w["w_down"],
               w["g_ple"], w["w_ple_gate"], w["w_ple"], g_final]
    acts = [o_a, r, gates, x, p]
    return pl.pallas_call(
        functools.partial(_post_kernel, final_norm=final_norm),
        grid=(t // tm,),
        in_specs=[pl.BlockSpec((tm, a.shape[1]), row) for a in acts]
        + [_const_spec(a.shape) for a in weights],
        out_specs=pl.BlockSpec((tm, D_MODEL), row),
        out_shape=jax.ShapeDtypeStruct((t, D_MODEL), F32),
        compiler_params=pltpu.CompilerParams(
            dimension_semantics=("parallel",), vmem_limit_bytes=VMEM_LIMIT_BYTES),
        name="post",
    )(*acts, *weights)


def kernel(x_prompt, x_sample, cache_k, cache_v, state_ret, page_table, p_prompt, p_sample,
           g_mix, w_in, sb_bias, w_sb_out, w_ret_out, w_o, ret_gn, g_ffn, w_up, w_down, g_ple,
           w_ple_gate, w_ple, g_final):
    depth = g_mix.shape[0]
    batch, seq, _ = x_prompt.shape
    dec_b, dec_l, _ = x_sample.shape
    past =page_table.shape[1] * PAGE_SIZE
    gain = lambda g: g.reshape(1, D_MODEL)

    hp = x_prompt.reshape(batch * seq, D_MODEL)
    hs = x_sample.reshape(dec_b * dec_l, D_MODEL)
    outs = {name: [] for name in ("kp", "vp", "sp", "ks", "vs", "ss")}
    for i in range(depth):
        last = i == depth - 1
        w_in_i = w_in[i].astype(BF16)
        w = {
            "w_sb_out": w_sb_out[i].astype(BF16), "w_ret_out": w_ret_out[i].astype(BF16),
            "w_o": w_o[i].astype(BF16), "g_ffn": gain(g_ffn[i]), "w_up": w_up[i].astype(BF16),
            "w_down": w_down[i].astype(BF16), "g_ple": gain(g_ple[i]),
            "w_ple_gate": w_ple_gate[i].astype(BF16), "w_ple": w_ple[i].astype(BF16),
        }

        branch, gates, k_t, v_t = _in_projection(hp, g_mix[i], w_in_i, BF16, head_major_seq=seq)
        branch3 = branch.reshape(batch, seq, BRANCH_WIDTH)
        o_a = _sb_prompt(branch3, sb_bias[i], batch, seq)
        r, s_fin = _ret_prompt(branch3, ret_gn[i], batch, seq)
        hp = _post(o_a.reshape(batch * seq, SB_WIDTH), r.reshape(batch * seq, RET_WIDTH), gates, hp,
                   p_prompt[i].reshape(batch * seq, -1), w, gain(g_final), last)
        outs["kp"].append(k_t.transpose(0, 3, 1, 2))
        outs["vp"].append(v_t.transpose(0, 3, 1, 2))
        outs["sp"].append(s_fin)

        branch, gates, k_a, v_a = _in_projection(hs, g_mix[i], w_in_i, F32)
        branch3 = branch.reshape(dec_b, dec_l, BRANCH_WIDTH)
        o_a = _sb_decode(branch3, cache_k, cache_v, i, page_table, sb_bias[i])
        r, s_new = _ret_decode(branch3, state_ret[i], ret_gn[i], past)
        hs = _post(o_a.reshape(dec_b * dec_l, SB_WIDTH), r.reshape(dec_b * dec_l, RET_WIDTH), gates,
                   hs, p_sample[i].reshape(dec_b * dec_l, -1), w, gain(g_final), last)
        outs["ks"].append(k_a.reshape(dec_b, dec_l, SB_HEADS, SB_DIM))
        outs["vs"].append(v_a.reshape(dec_b, dec_l, SB_HEADS, SB_DIM))
        outs["ss"].append(s_new)

    return (hp.reshape(batch, seq, D_MODEL), hs.reshape(dec_b, dec_l, D_MODEL),
            jnp.stack(outs["kp"]), jnp.stack(outs["vp"]), jnp.stack(outs["sp"]),
            jnp.stack(outs["ks"]), jnp.stack(outs["vs"]), jnp.stack(outs["ss"]))
```

```python
import functools

import jax
import jax.numpy as jnp
import numpy as np
from jax import lax
from jax.experimental import pallas as pl
from jax.experimental.pallas import tpu as pltpu

F32 = jnp.float32
BF16 = jnp.bfloat16

D_MODEL = 1024
SB_HEADS = 8
SB_DIM = 64
SB_WIDTH = SB_HEADS * SB_DIM
RET_HEADS = 4
RET_DIM = 128
RET_WIDTH = RET_HEADS * RET_DIM
D_FF = 4 * D_MODEL
PAGE_SIZE = 128
ROPE_BASE = 10000.0
EPS = 1e-6
SB_SCALE = SB_DIM ** -0.5
RET_SCALE = RET_DIM ** -0.5
EXP_CLAMP = 80.0

BRANCH_COLS = 512
COL_QA, COL_KA, COL_VA, COL_QR, COL_KR, COL_VR, COL_GR = range(7)
N_BRANCH_BLOCKS = 7
BRANCH_WIDTH = N_BRANCH_BLOCKS * BRANCH_COLS
GATE_WIDTH = 2 * D_MODEL

LANES = 128
VMEM_LIMIT_BYTES = 56 * 1024 * 1024

PROJ_ROWS = 512
POST_ROWS = 256
SB_BLOCK = 256
RET_CHUNK = 256
DEC_PAGES = 32
DEC_SEQS = 8


def _const_spec(shape):
    return pl.BlockSpec(shape, lambda *_: (0,) * len(shape), pipeline_mode=pl.Buffered(1))


def _rms(x, g):
    return x * lax.rsqrt(jnp.mean(x * x, axis=-1, keepdims=True) + EPS) * g


def _dot(a, b):
    return jnp.dot(a, b, preferred_element_type=F32)


def _dot_nt(a, b):
    return lax.dot_general(a, b, (((1,), (1,)), ((), ())), preferred_element_type=F32)


def _dot_tn(a, b):
    return lax.dot_general(a, b, (((0,), (0,)), ((), ())), preferred_element_type=F32)


def _proj_kernel(x_ref, g_ref, w_ref, branch_ref, gates_ref, k_ref, v_ref, *, head_major_kv):
    u = _rms(x_ref[...], g_ref[...]).astype(BF16)
    for c in range(N_BRANCH_BLOCKS):
        cols = slice(c * BRANCH_COLS, (c + 1) * BRANCH_COLS)
        r = _dot(u, w_ref[:, cols])
        if c == COL_QA:
            r = r * SB_SCALE
        if c in (COL_KA, COL_VA):
            kv_ref = k_ref if c == COL_KA else v_ref
            if head_major_kv:
                kv_ref[0] = r.T.reshape(SB_HEADS, SB_DIM, r.shape[0])
            else:
                kv_ref[...] = r
        branch_ref[:, cols] = r.astype(branch_ref.dtype)
    for c in range(GATE_WIDTH // BRANCH_COLS):
        cols = slice(c * BRANCH_COLS, (c + 1) * BRANCH_COLS)
        wcols = slice(BRANCH_WIDTH + c * BRANCH_COLS, BRANCH_WIDTH + (c + 1) * BRANCH_COLS)
        gates_ref[:, cols] = _dot(u, w_ref[:, wcols]).astype(gates_ref.dtype)


def _in_projection(x, g_mix, w_in_bf16, act_dtype, head_major_seq=None):
    t = x.shape[0]
    tm = PROJ_ROWS
    row = lambda i: (i, 0)
    if head_major_seq is None:
        kv_spec = pl.BlockSpec((tm, SB_WIDTH), row)
        kv_shape = jax.ShapeDtypeStruct((t, SB_WIDTH), F32)
    else:
        tiles = head_major_seq // tm
        kv_spec = pl.BlockSpec((1, SB_HEADS, SB_DIM, tm), lambda i: (i // tiles, 0, 0, i % tiles))
        kv_shape = jax.ShapeDtypeStruct((t // head_major_seq, SB_HEADS, SB_DIM, head_major_seq), F32)
    return pl.pallas_call(
        functools.partial(_proj_kernel, head_major_kv=head_major_seq is not None),
        grid=(t // tm,),
        in_specs=[
            pl.BlockSpec((tm, D_MODEL), row),
            _const_spec((1, D_MODEL)),
            _const_spec(w_in_bf16.shape),
        ],
        out_specs=[
            pl.BlockSpec((tm, BRANCH_WIDTH), row),
            pl.BlockSpec((tm, GATE_WIDTH), row),
            kv_spec,
            kv_spec,
        ],
        out_shape=[
            jax.ShapeDtypeStruct((t, BRANCH_WIDTH), act_dtype),
            jax.ShapeDtypeStruct((t, GATE_WIDTH), act_dtype),
            kv_shape,
            kv_shape,
        ],
        compiler_params=pltpu.CompilerParams(
            dimension_semantics=("parallel",), vmem_limit_bytes=VMEM_LIMIT_BYTES),
        name="in_projection",
    )(x, g_mix.reshape(1, D_MODEL), w_in_bf16)


def _stick_weights(z, tri, carry, mask=None):
    m, n = z.shape
    t = tri.shape[0]
    stay = jnp.maximum(z, jnp.log(1.0 + jnp.exp(jnp.minimum(z, EXP_CLAMP))))
    chunks = [stay[:, c * t:(c + 1) * t] for c in range(n // t)]
    if mask is not None:
        chunks[-1] = jnp.where(mask, chunks[-1], 0.0)
    right = _dot(jnp.concatenate(chunks, axis=0).astype(BF16), tri)
    weights = [None] * len(chunks)
    for c in reversed(range(len(chunks))):
        carry_t = jnp.concatenate([carry] * (t // LANES), axis=1)
        a = jnp.exp(z[:, c * t:(c + 1) * t] - chunks[c] - right[c * m:(c + 1) * m] - carry_t)
        weights[c] = jnp.where(mask, a, 0.0) if mask is not None and c == len(chunks) - 1 else a
        carry = carry + jnp.broadcast_to(jnp.sum(chunks[c], axis=1, keepdims=True), (m, LANES))
    return jnp.concatenate(weights, axis=1), carry


def _sb_prompt_kernel(bias_ref, tri_ref, q_ref, k_ref, v_ref, o_ref, acc_ref, carry_ref):
    blk = SB_BLOCK
    i = pl.program_id(1)
    row = lax.broadcasted_iota(jnp.int32, (blk, blk), 0)
    col = lax.broadcasted_iota(jnp.int32, (blk, blk), 1)
    causal = col < row
    acc_ref[...] = jnp.zeros_like(acc_ref)
    carry_ref[...] = jnp.zeros_like(carry_ref)

    def key_blocks(first, n_blocks, mask=None):
        start = pl.multiple_of(first * blk, n_blocks * blk)
        tri = tri_ref[...]
        low_q = lax.broadcasted_iota(jnp.int32, (blk, LANES), 1) < SB_DIM
        low_v = lax.broadcasted_iota(jnp.int32, (n_blocks * blk, LANES), 1) < SB_DIM
        for pair in range(SB_HEADS // 2):
            cols = slice(pair * LANES, (pair + 1) * LANES)
            q2 = q_ref[0, :, cols]
            k2 = k_ref[0, pl.ds(start, n_blocks * blk), cols]
            v2 = v_ref[0, pl.ds(start, n_blocks * blk), cols]
            update = None
            for sub in range(2):
                head = 2 * pair + sub
                q_head = jnp.where(low_q if sub == 0 else ~low_q, q2, jnp.zeros_like(q2))
                v_head = jnp.where(low_v if sub == 0 else ~low_v, v2, jnp.zeros_like(v2))
                z = _dot_nt(q_head, k2) + bias_ref[head]
                a, carry_ref[head] = _stick_weights(z, tri, carry_ref[head], mask)
                d = _dot(a.astype(BF16), v_head)
                update = d if update is None else update + d
            acc_ref[pair] += update

    @pl.when(i % 2 == 0)
    def _():
        key_blocks(i, 1, causal)

    @pl.when(i % 2 == 1)
    def _():
        key_blocks(i - 1, 2, causal)

    def body(t, c):
        key_blocks((i // 2 - 1 - t) * 2, 2)
        return c

    lax.fori_loop(0, i // 2, body, 0)
    for pair in range(SB_HEADS // 2):
        o_ref[0, :, pair * LANES:(pair + 1) * LANES] = acc_ref[pair].astype(o_ref.dtype)


def _tri(n):
    idx = jnp.arange(n)
    return (idx[:, None] > idx[None, :]).astype(BF16)


def _sb_prompt(branch, sb_bias, batch, seq):
    blk = SB_BLOCK
    return pl.pallas_call(
        _sb_prompt_kernel,
        grid=(batch, seq // blk),
        in_specs=[
            pl.BlockSpec(memory_space=pltpu.SMEM),
            _const_spec((blk, blk)),
            pl.BlockSpec((1, blk, SB_WIDTH), lambda b, i: (b, i, COL_QA)),
            pl.BlockSpec((1, seq, SB_WIDTH), lambda b, i: (b, 0, COL_KA)),
            pl.BlockSpec((1, seq, SB_WIDTH), lambda b, i: (b, 0, COL_VA)),
        ],
        out_specs=pl.BlockSpec((1, blk, SB_WIDTH), lambda b, i: (b, i, 0)),
        out_shape=jax.ShapeDtypeStruct((batch, seq, SB_WIDTH), BF16),
        scratch_shapes=[
            pltpu.VMEM((SB_HEADS // 2, blk, LANES), F32),
            pltpu.VMEM((SB_HEADS, blk, LANES), F32),
        ],
        compiler_params=pltpu.CompilerParams(
            dimension_semantics=("parallel", "arbitrary"), vmem_limit_bytes=VMEM_LIMIT_BYTES),
        name="sb_prompt",
    )(sb_bias, _tri(blk), branch, branch, branch)


def _sb_decode_kernel(pt_ref, bias_ref, new_bias_ref, tri_ref, q_ref, kn_ref, vn_ref, *rest):
    del pt_ref
    k_pages = rest[:DEC_PAGES]
    v_pages = rest[DEC_PAGES:2 * DEC_PAGES]
    o_ref, qbd_ref, kpad_ref, vpad_ref, acc_ref, carry_ref = rest[2 * DEC_PAGES:]
    j = pl.program_id(1)
    dec_l = q_ref.shape[1]
    rows = SB_HEADS * dec_l

    @pl.when(j == 0)
    def _():
        q = q_ref[0]
        r_head = lax.broadcasted_iota(jnp.int32, (rows, SB_WIDTH), 0) // dec_l
        c_head = lax.broadcasted_iota(jnp.int32, (rows, SB_WIDTH), 1) // SB_DIM
        qbd_ref[...] = jnp.where(r_head == c_head, jnp.concatenate([q] * SB_HEADS, axis=0), 0.0)
        kpad_ref[...] = jnp.zeros_like(kpad_ref)
        vpad_ref[...] = jnp.zeros_like(vpad_ref)
        kpad_ref[0:dec_l, :] = kn_ref[0]
        vpad_ref[0:dec_l, :] = vn_ref[0]
        z = _dot_nt(qbd_ref[...], kpad_ref[...]) + new_bias_ref[...]
        a, carry_ref[...] = _stick_weights(
            z, tri_ref[:PAGE_SIZE, :PAGE_SIZE], jnp.zeros((rows, LANES), F32))
        acc_ref[...] = _dot(a, vpad_ref[...])

    kt = jnp.concatenate([r[0, 0].reshape(SB_WIDTH, PAGE_SIZE) for r in k_pages], axis=1)
    vt = jnp.concatenate([r[0, 0].reshape(SB_WIDTH, PAGE_SIZE) for r in v_pages], axis=1)
    z = _dot(qbd_ref[...], kt) + jnp.concatenate([bias_ref[...]] * DEC_PAGES, axis=1)
    a, carry_ref[...] = _stick_weights(z, tri_ref[...], carry_ref[...])
    acc_ref[...] += _dot_nt(a, vt)

    @pl.when(j == pl.num_programs(1) - 1)
    def _():
        r_head = lax.broadcasted_iota(jnp.int32, (rows, SB_WIDTH), 0) // dec_l
        c_head = lax.broadcasted_iota(jnp.int32, (rows, SB_WIDTH), 1) // SB_DIM
        own = jnp.where(r_head == c_head, acc_ref[...], 0.0)
        out = own[0:dec_l]
        for h in range(1, SB_HEADS):
            out = out + own[h * dec_l:(h + 1) * dec_l]
        o_ref[0] = out


def _sb_decode(branch, cache_k, cache_v, layer, page_table, sb_bias):
    dec_b, dec_l, _ = branch.shape
    n_pages = page_table.shape[1]
    n_steps = n_pages // DEC_PAGES
    rows = SB_HEADS * dec_l
    tri_n = 2 * PAGE_SIZE

    cache_kt = cache_k.transpose(0, 1, 3, 4, 2)
    cache_vt = cache_v.transpose(0, 1, 3, 4, 2)

    row_head = jnp.arange(rows) // dec_l
    row_query = jnp.arange(rows) % dec_l
    row_bias = jnp.broadcast_to(sb_bias[row_head][:, None], (rows, PAGE_SIZE))
    new_bias = jnp.where(jnp.arange(PAGE_SIZE)[None, :] < row_query[:, None], row_bias, -1e30)

    def page_spec(p):
        def index(b, j, pt):
            return (layer, pt[b, (n_steps - 1 - j) * DEC_PAGES + p], 0, 0, 0)
        return pl.BlockSpec((1, 1, SB_HEADS, SB_DIM, PAGE_SIZE), index)

    const = lambda shape: pl.BlockSpec(shape, lambda b, j, pt: (0,) * len(shape),
                                       pipeline_mode=pl.Buffered(1))
    grid_spec = pltpu.PrefetchScalarGridSpec(
        num_scalar_prefetch=1,
        grid=(dec_b, n_steps),
        in_specs=[
            const((rows, PAGE_SIZE)),
            const((rows, PAGE_SIZE)),
            const((tri_n, tri_n)),
            pl.BlockSpec((1, dec_l, SB_WIDTH), lambda b, j, pt: (b, 0, COL_QA)),
            pl.BlockSpec((1, dec_l, SB_WIDTH), lambda b, j, pt: (b, 0, COL_KA)),
            pl.BlockSpec((1, dec_l, SB_WIDTH), lambda b, j, pt: (b, 0, COL_VA)),
        ] + [page_spec(p) for p in range(DEC_PAGES)] * 2,
        out_specs=pl.BlockSpec((1, dec_l, SB_WIDTH), lambda b, j, pt: (b, 0, 0)),
        scratch_shapes=[
            pltpu.VMEM((rows, SB_WIDTH), F32),
            pltpu.VMEM((PAGE_SIZE, SB_WIDTH), F32),
            pltpu.VMEM((PAGE_SIZE, SB_WIDTH), F32),
            pltpu.VMEM((rows, SB_WIDTH), F32),
            pltpu.VMEM((rows, LANES), F32),
        ],
    )
    return pl.pallas_call(
        _sb_decode_kernel,
        grid_spec=grid_spec,
        out_shape=jax.ShapeDtypeStruct((dec_b, dec_l, SB_WIDTH), F32),
        compiler_params=pltpu.CompilerParams(
            dimension_semantics=("parallel", "arbitrary"), vmem_limit_bytes=VMEM_LIMIT_BYTES),
        name="sb_decode",
    )(page_table, row_bias, new_bias, _tri(tri_n), branch, branch, branch,
      *([cache_kt] * DEC_PAGES), *([cache_vt] * DEC_PAGES))


def _log_gamma():
    return jnp.log(1.0 - jnp.exp2(-5.0 - jnp.arange(RET_HEADS, dtype=F32)))


def _rope_tables(pos):
    half = RET_DIM // 2
    freqs = ROPE_BASE ** (-jnp.arange(half, dtype=F32) / half)
    ang = pos.astype(F32)[:, None] * freqs[None, :]
    cos, sin = jnp.cos(ang), jnp.sin(ang)
    return jnp.concatenate([cos, cos], axis=1), jnp.concatenate([-sin, sin], axis=1)


def _decay_tables(length):
    lg = _log_gamma()
    idx = jnp.arange(length, dtype=F32)
    rel = idx[:, None] - idx[None, :]
    intra = jnp.where(rel[None] >= 0, jnp.exp(jnp.maximum(rel, 0.0)[None] * lg[:, None, None]), 0.0)
    q_decay = jnp.exp((idx + 1.0)[None, :] * lg[:, None])
    k_decay = jnp.exp((length - 1.0 - idx)[None, :] * lg[:, None])
    rep = lambda t: jnp.broadcast_to(t[:, :, None], (RET_HEADS, length, LANES))
    return intra, rep(q_decay), rep(k_decay), jnp.exp(length * lg)


def _rope(x, cos2, sin2):
    return x * cos2 + pltpu.roll(x, RET_DIM // 2, axis=1) * sin2


def _groupnorm_gate(o, gn, g):
    mu = jnp.mean(o, axis=-1, keepdims=True)
    d = o - mu
    var = jnp.mean(d * d, axis=-1, keepdims=True)
    return jax.nn.silu(g) * (d * lax.rsqrt(var + EPS) * gn)


def _ret_prompt_kernel(sdec_ref, cos_ref, sin_ref, intra_ref, qdec_ref, kdec_ref, gn_ref,
                       q_ref, k_ref, v_ref, g_ref, r_ref, sfin_ref, state_ref):
    c = pl.program_id(1)

    @pl.when(c == 0)
    def _():
        state_ref[...] = jnp.zeros_like(state_ref)

    cos2, sin2 = cos_ref[...], sin_ref[...]
    for h in range(RET_HEADS):
        cols = slice(h * RET_DIM, (h + 1) * RET_DIM)
        q = _rope(q_ref[0, :, cols].astype(F32), cos2, sin2)
        k = _rope(k_ref[0, :, cols].astype(F32), cos2, sin2) * RET_SCALE
        v = v_ref[0, :, cols]
        qb = q.astype(BF16)
        state = state_ref[h]
        scores = _dot_nt(qb, k.astype(BF16)) * intra_ref[h]
        o = _dot(scores.astype(BF16), v) + _dot(qb, state.astype(BF16)) * qdec_ref[h]
        state_ref[h] = sdec_ref[h] * state + _dot_tn((k * kdec_ref[h]).astype(BF16), v)
        r_ref[0, :, cols] = _groupnorm_gate(
            o, gn_ref[h], g_ref[0, :, cols].astype(F32)).astype(r_ref.dtype)

    @pl.when(c == pl.num_programs(1) - 1)
    def _():
        sfin_ref[0] = state_ref[...]


def _ret_prompt(branch, ret_gn, batch, seq):
    ch = RET_CHUNK
    cos2, sin2 = _rope_tables(jnp.arange(seq, dtype=jnp.int32))
    intra, q_decay, k_decay, s_decay = _decay_tables(ch)
    col = lambda cb: (lambda b, c: (b, c, cb))
    blk = (1, ch, RET_WIDTH)
    return pl.pallas_call(
        _ret_prompt_kernel,
        grid=(batch, seq // ch),
        in_specs=[
            pl.BlockSpec(memory_space=pltpu.SMEM),
            pl.BlockSpec((ch, RET_DIM), lambda b, c: (c, 0)),
            pl.BlockSpec((ch, RET_DIM), lambda b, c: (c, 0)),
            _const_spec((RET_HEADS, ch, ch)),
            _const_spec((RET_HEADS, ch, LANES)),
            _const_spec((RET_HEADS, ch, LANES)),
            _const_spec((RET_HEADS, 1, RET_DIM)),
            pl.BlockSpec(blk, col(COL_QR)),
            pl.BlockSpec(blk, col(COL_KR)),
            pl.BlockSpec(blk, col(COL_VR)),
            pl.BlockSpec(blk, col(COL_GR)),
        ],
        out_specs=[
            pl.BlockSpec(blk, lambda b, c: (b, c, 0)),
            pl.BlockSpec((1, RET_HEADS, RET_DIM, RET_DIM), lambda b, c: (b, 0, 0, 0)),
        ],
        out_shape=[
            jax.ShapeDtypeStruct((batch, seq, RET_WIDTH), BF16),
            jax.ShapeDtypeStruct((batch, RET_HEADS, RET_DIM, RET_DIM), F32),
        ],
        scratch_shapes=[pltpu.VMEM((RET_HEADS, RET_DIM, RET_DIM), F32)],
        compiler_params=pltpu.CompilerParams(
            dimension_semantics=("parallel", "arbitrary"), vmem_limit_bytes=VMEM_LIMIT_BYTES),
        name="ret_prompt",
    )(s_decay, cos2, sin2, intra, q_decay, k_decay, ret_gn.reshape(RET_HEADS, 1, RET_DIM),
      branch, branch, branch, branch)


def _ret_decode_kernel(sdec_ref, cos_ref, sin_ref, intra_ref, qdec_ref, kdec_ref, gn_ref,
                       q_ref, k_ref, v_ref, g_ref, s_ref, r_ref, snew_ref):
    n_seq, dec_l, _ = q_ref.shape
    cos2, sin2 = cos_ref[...], sin_ref[...]

    for s in range(n_seq):
        for h in range(RET_HEADS):
            cols = slice(h * RET_DIM, (h + 1) * RET_DIM)
            q = _rope(q_ref[s, :, cols], cos2, sin2)
            k = _rope(k_ref[s, :, cols], cos2, sin2) * RET_SCALE
            v = v_ref[s, :, cols].astype(BF16)
            qb = q.astype(BF16)
            state = s_ref[s, h]
            scores = _dot_nt(qb, k.astype(BF16)) * intra_ref[h]
            o = _dot(scores.astype(BF16), v) + _dot(qb, state.astype(BF16)) * qdec_ref[h]
            snew_ref[s, h] = sdec_ref[h] * state + _dot_tn((k * kdec_ref[h]).astype(BF16), v)
            r_ref[s, :, cols] = _groupnorm_gate(o, gn_ref[h], g_ref[s, :, cols])


def _ret_decode(branch, state, ret_gn, past):
    dec_b, dec_l, _ = branch.shape
    g = DEC_SEQS
    cos2, sin2 = _rope_tables(past + jnp.arange(dec_l, dtype=jnp.int32))
    intra, q_decay, k_decay, s_decay = _decay_tables(dec_l)
    col = lambda cb: (lambda i: (i, 0, cb))
    blk = (g, dec_l, RET_WIDTH)
    sblk = (g, RET_HEADS, RET_DIM, RET_DIM)
    return pl.pallas_call(
        _ret_decode_kernel,
        grid=(dec_b // g,),
        in_specs=[
            pl.BlockSpec(memory_space=pltpu.SMEM),
            _const_spec((dec_l, RET_DIM)),
            _const_spec((dec_l, RET_DIM)),
            _const_spec((RET_HEADS, dec_l, dec_l)),
            _const_spec((RET_HEADS, dec_l, LANES)),
            _const_spec((RET_HEADS, dec_l, LANES)),
            _const_spec((RET_HEADS, 1, RET_DIM)),
            pl.BlockSpec(blk, col(COL_QR)),
            pl.BlockSpec(blk, col(COL_KR)),
            pl.BlockSpec(blk, col(COL_VR)),
            pl.BlockSpec(blk, col(COL_GR)),
            pl.BlockSpec(sblk, lambda i: (i, 0, 0, 0)),
        ],
        out_specs=[
            pl.BlockSpec(blk, lambda i: (i, 0, 0)),
            pl.BlockSpec(sblk, lambda i: (i, 0, 0, 0)),
        ],
        out_shape=[
            jax.ShapeDtypeStruct((dec_b, dec_l, RET_WIDTH), F32),
            jax.ShapeDtypeStruct(state.shape, state.dtype),
        ],
        compiler_params=pltpu.CompilerParams(
            dimension_semantics=("parallel",), vmem_limit_bytes=VMEM_LIMIT_BYTES),
        name="ret_decode",
    )(s_decay, cos2, sin2, intra, q_decay, k_decay, ret_gn.reshape(RET_HEADS, 1, RET_DIM),
      branch, branch, branch, branch, state)


def _post_kernel(oa_ref, r_ref, gates_ref, x_ref, p_ref, wsb_ref, wret_ref, wo_ref, gffn_ref,
                 wup_ref, wdown_ref, gple_ref, wpg_ref, wple_ref, gfin_ref, y_ref, *, final_norm):
    y_a = _dot(oa_ref[...].astype(BF16), wsb_ref[...])
    y_r = _dot(r_ref[...].astype(BF16), wret_ref[...])
    gate_a = gates_ref[:, :D_MODEL].astype(F32)
    gate_r = gates_ref[:, D_MODEL:].astype(F32)
    merged = jax.nn.sigmoid(gate_a) * y_a + jax.nn.sigmoid(gate_r) * y_r
    h = x_ref[...] + _dot(merged.astype(BF16), wo_ref[...])

    u = _rms(h, gffn_ref[...]).astype(BF16)
    for c in range(D_FF // D_MODEL):
        cols = slice(c * D_MODEL, (c + 1) * D_MODEL)
        act = jnp.square(jnp.maximum(_dot(u, wup_ref[:, cols]), 0.0))
        h = h + _dot(act.astype(BF16), wdown_ref[cols, :])

    u = _rms(h, gple_ref[...]).astype(BF16)
    gate = jax.nn.sigmoid(_dot(u, wpg_ref[...]))
    h = h + gate * _dot(p_ref[...].astype(BF16), wple_ref[...])
    y_ref[...] = _rms(h, gfin_ref[...]) if final_norm else h


def _post(o_a, r, gates, x, p, w, g_final, final_norm):
    t = x.shape[0]
    tm = POST_ROWS
    row = lambda i: (i, 0)
    weights = [w["w_sb_out"], w["w_ret_out"], w["w_o"], w["g_ffn"], w["w_up"], w["w_down"],
               w["g_ple"], w["w_ple_gate"], w["w_ple"], g_final]
    acts = [o_a, r, gates, x, p]
    return pl.pallas_call(
        functools.partial(_post_kernel, final_norm=final_norm),
        grid=(t // tm,),
        in_specs=[pl.BlockSpec((tm, a.shape[1]), row) for a in acts]
        + [_const_spec(a.shape) for a in weights],
        out_specs=pl.BlockSpec((tm, D_MODEL), row),
        out_shape=jax.ShapeDtypeStruct((t, D_MODEL), F32),
        compiler_params=pltpu.CompilerParams(
            dimension_semantics=("parallel",), vmem_limit_bytes=VMEM_LIMIT_BYTES),
        name="post",
    )(*acts, *weights)


def kernel(x_prompt, x_sample, cache_k, cache_v, state_ret, page_table, p_prompt, p_sample,
           g_mix, w_in, sb_bias, w_sb_out, w_ret_out, w_o, ret_gn, g_ffn, w_up, w_down, g_ple,
           w_ple_gate, w_ple, g_final):
    depth = g_mix.shape[0]
    batch, seq, _ = x_prompt.shape
    dec_b, dec_l, _ = x_sample.shape
    past =page_table.shape[1] * PAGE_SIZE
    gain = lambda g: g.reshape(1, D_MODEL)

    hp = x_prompt.reshape(batch * seq, D_MODEL)
    hs = x_sample.reshape(dec_b * dec_l, D_MODEL)
    outs = {name: [] for name in ("kp", "vp", "sp", "ks", "vs", "ss")}
    for i in range(depth):
        last = i == depth - 1
        w_in_i = w_in[i].astype(BF16)
        w = {
            "w_sb_out": w_sb_out[i].astype(BF16), "w_ret_out": w_ret_out[i].astype(BF16),
            "w_o": w_o[i].astype(BF16), "g_ffn": gain(g_ffn[i]), "w_up": w_up[i].astype(BF16),
            "w_down": w_down[i].astype(BF16), "g_ple": gain(g_ple[i]),
            "w_ple_gate": w_ple_gate[i].astype(BF16), "w_ple": w_ple[i].astype(BF16),
        }

        branch, gates, k_t, v_t = _in_projection(hp, g_mix[i], w_in_i, BF16, head_major_seq=seq)
        branch3 = branch.reshape(batch, seq, BRANCH_WIDTH)
        o_a = _sb_prompt(branch3, sb_bias[i], batch, seq)
        r, s_fin = _ret_prompt(branch3, ret_gn[i], batch, seq)
        hp = _post(o_a.reshape(batch * seq, SB_WIDTH), r.reshape(batch * seq, RET_WIDTH), gates, hp,
                   p_prompt[i].reshape(batch * seq, -1), w, gain(g_final), last)
        outs["kp"].append(k_t.transpose(0, 3, 1, 2))
        outs["vp"].append(v_t.transpose(0, 3, 1, 2))
        outs["sp"].append(s_fin)

        branch, gates, k_a, v_a = _in_projection(hs, g_mix[i], w_in_i, F32)
        branch3 = branch.reshape(dec_b, dec_l, BRANCH_WIDTH)
        o_a = _sb_decode(branch3, cache_k, cache_v, i, page_table, sb_bias[i])
        r, s_new = _ret_decode(branch3, state_ret[i], ret_gn[i], past)
        hs = _post(o_a.reshape(dec_b * dec_l, SB_WIDTH), r.reshape(dec_b * dec_l, RET_WIDTH), gates,
                   hs, p_sample[i].reshape(dec_b * dec_l, -1), w, gain(g_final), last)
        outs["ks"].append(k_a.reshape(dec_b, dec_l, SB_HEADS, SB_DIM))
        outs["vs"].append(v_a.reshape(dec_b, dec_l, SB_HEADS, SB_DIM))
        outs["ss"].append(s_new)

    return (hp.reshape(batch, seq, D_MODEL), hs.reshape(dec_b, dec_l, D_MODEL),
            jnp.stack(outs["kp"]), jnp.stack(outs["vp"]), jnp.stack(outs["sp"]),
            jnp.stack(outs["ks"]), jnp.stack(outs["vs"]), jnp.stack(outs["ss"]))
```
